```python
import jax, jax.numpy as jnp
from jax import lax
import numpy as np


D_MODEL = 2048
BATCH = 8
SEQ = 2048
DEPTH = 1

POOL_GROUPS = 4
POOL_GROUP_DIM = 256
POOL_WIDTH = POOL_GROUPS * POOL_GROUP_DIM
POOL_WINDOWS = (2, 4, 8, 16)
N_HEADS = 8
HEAD_DIM = 128
V_HEAD_DIM = 2 * HEAD_DIM
QK_WIDTH = N_HEADS * 2 * HEAD_DIM
ATTN_WIDTH = N_HEADS * V_HEAD_DIM
ROT_DIM = HEAD_DIM // 4
ROT_HALF = ROT_DIM // 2
ROPE_THETA = 500000.0
Q_BLOCK = 128
ATTN_SCALE = HEAD_DIM ** -0.5
SUBLN_EPS = 1e-5
IN_WIDTH = POOL_WIDTH + 2 * QK_WIDTH + ATTN_WIDTH + 2 * D_MODEL
IN_SPLITS = (POOL_WIDTH,
             POOL_WIDTH + QK_WIDTH,
             POOL_WIDTH + 2 * QK_WIDTH,
             POOL_WIDTH + 2 * QK_WIDTH + ATTN_WIDTH,
             POOL_WIDTH + 2 * QK_WIDTH + ATTN_WIDTH + D_MODEL)
N_EXPERTS = 64
EXPERT_DIM = 512
SHARED_DIM = 512
TOP_K = 8
N_GROUPS = 8
TOPK_GROUPS = 4
EXPERTS_PER_GROUP = N_EXPERTS // N_GROUPS
ROUTED_SCALE = 2.5
MOE_BLOCK = 128
N_MOD = 6
NORM_EPS = 1e-6

kernel_name = "hybrid_pool_diffattn_moe_adaln_block"


def rms_norm(x, w, eps=NORM_EPS):
    xf = x.astype(jnp.float32)
    y = xf * lax.rsqrt(jnp.mean(xf * xf, axis=-1, keepdims=True) + eps)
    return (y * w.astype(jnp.float32)).astype(x.dtype)


def rotary_tables(seq_len):
    pos = jnp.arange(seq_len, dtype=jnp.float32)
    inv_freq = 1.0 / (ROPE_THETA ** (jnp.arange(0, ROT_DIM, 2, dtype=jnp.float32) / ROT_DIM))
    ang = pos[:, None] * inv_freq[None, :]
    return jnp.cos(ang), jnp.sin(ang)


def partial_rope(t, cos, sin):
    tf = t.astype(jnp.float32)
    r1 = tf[..., :ROT_HALF]
    r2 = tf[..., ROT_HALF:ROT_DIM]
    cb = cos[:, None, None, :]
    sb = sin[:, None, None, :]
    out = jnp.concatenate([r1 * cb - r2 * sb, r2 * cb + r1 * sb, tf[..., ROT_DIM:]], axis=-1)
    return out.astype(t.dtype)


def pool_mixer(u, pool_w, pool_scale):
    b_, s_, _ = u.shape
    ug = u.reshape(b_, s_, POOL_GROUPS, POOL_GROUP_DIM).astype(jnp.float32)
    cs = jnp.concatenate([jnp.zeros_like(ug[:, :1]), jnp.cumsum(ug, axis=1)], axis=1)
    t = jnp.arange(s_)[:, None]
    win = jnp.array(POOL_WINDOWS, dtype=jnp.int32)[None, :]
    lo = jnp.maximum(t + 1 - win, 0)
    cnt = jnp.minimum(t + 1, win).astype(jnp.float32)
    cs_lo = cs[:, lo, jnp.arange(POOL_GROUPS)[None, :], :]
    mean = (cs[:, 1:] - cs_lo) / cnt[None, :, :, None]
    mixed = (mean - ug).astype(u.dtype)
    y = jnp.einsum('bsgc,gce->bsge', mixed, pool_w) * pool_scale
    return y.reshape(b_, s_, POOL_WIDTH)


def diff_attention(q, k, v, lq1, lk1, lq2, lk2, subln_w, lambda_init, cos, sin):
    b_, s_, _ = q.shape
    q = partial_rope(q.reshape(b_, s_, N_HEADS, 2, HEAD_DIM), cos, sin)
    k = partial_rope(k.reshape(b_, s_, N_HEADS, 2, HEAD_DIM), cos, sin)
    v = v.reshape(b_, s_, N_HEADS, V_HEAD_DIM)
    lam = (jnp.exp(jnp.sum(lq1.astype(jnp.float32) * lk1.astype(jnp.float32)))
           - jnp.exp(jnp.sum(lq2.astype(jnp.float32) * lk2.astype(jnp.float32)))
           + lambda_init)
    nq = s_ // Q_BLOCK
    q_blocks = jnp.moveaxis(q.reshape(b_, nq, Q_BLOCK, N_HEADS, 2, HEAD_DIM), 1, 0)
    q_pos = jnp.arange(s_).reshape(nq, Q_BLOCK)
    k_pos = jnp.arange(s_)

    def attend(args):
        qb, qp = args
        s = jnp.einsum('bqhcd,bkhcd->cbhqk', qb, k, preferred_element_type=jnp.float32) * ATTN_SCALE
        s = jnp.where(qp[:, None] >= k_pos[None, :], s, -jnp.inf)
        p = jax.nn.softmax(s, axis=-1)
        a = (p[0] - lam * p[1]).astype(v.dtype)
        return jnp.einsum('bhqk,bkhe->bqhe', a, v)

    o = lax.map(attend, (q_blocks, q_pos))
    o = jnp.moveaxis(o, 0, 1).reshape(b_, s_, N_HEADS, V_HEAD_DIM)
    o = rms_norm(o, subln_w, SUBLN_EPS) * (1.0 - lambda_init)
    return o.reshape(b_, s_, ATTN_WIDTH)


def hybrid_mixer(h, w_in, pool_w, pool_scale, w_branch_pool, lq1, lk1, lq2, lk2,
                 subln_w, w_branch_attn, w_out, lambda_init, cos, sin):
    proj = jnp.einsum('bsd,de->bse', h, w_in)
    u, q, k, v, g_pool, g_attn = jnp.split(proj, IN_SPLITS, axis=-1)
    y_pool = jnp.einsum('bsp,pd->bsd', pool_mixer(u, pool_w, pool_scale), w_branch_pool)
    y_attn = jnp.einsum('bsa,ad->bsd',
                        diff_attention(q, k, v, lq1, lk1, lq2, lk2, subln_w, lambda_init, cos, sin),
                        w_branch_attn)
    merged = jax.nn.sigmoid(g_pool) * y_pool + jax.nn.sigmoid(g_attn) * y_attn
    return jnp.einsum('bsd,de->bse', merged, w_out)


def swiglu(x, wg, wu, wd):
    return (jax.nn.silu(x @ wg) * (x @ wu)) @ wd


def moe_ffn(h, w_router, router_bias, w_gate_e, w_up_e, w_down_e, w_sh_gate, w_sh_up, w_sh_down):
    b_, s_, d_ = h.shape
    n = b_ * s_
    xf = h.reshape(n, d_)
    scores = jax.nn.sigmoid(jnp.einsum('nd,de->ne', xf, w_router, preferred_element_type=jnp.float32))
    biased = scores + router_bias.astype(jnp.float32)
    grp_score = lax.top_k(biased.reshape(n, N_GROUPS, EXPERTS_PER_GROUP), 2)[0].sum(-1)
    _, top_groups = lax.top_k(grp_score, TOPK_GROUPS)
    group_mask = jnp.any(top_groups[..., None] == jnp.arange(N_GROUPS), axis=-2)
    expert_mask = jnp.repeat(group_mask, EXPERTS_PER_GROUP, axis=-1)
    _, expert_idx = lax.top_k(jnp.where(expert_mask, biased, -jnp.inf), TOP_K)
    sel = jnp.take_along_axis(scores, expert_idx, axis=-1)
    gates = sel / jnp.sum(sel, axis=-1, keepdims=True) * ROUTED_SCALE

    n_assign = n * TOP_K
    n_slots = (n_assign + N_EXPERTS * (MOE_BLOCK - 1) + MOE_BLOCK - 1) // MOE_BLOCK * MOE_BLOCK
    n_blocks = n_slots // MOE_BLOCK
    e_flat = expert_idx.reshape(-1)
    tok_flat = jnp.repeat(jnp.arange(n, dtype=jnp.int32), TOP_K)
    g_flat = gates.reshape(-1)
    order = jnp.argsort(e_flat)
    e_sorted = e_flat[order]
    counts = jnp.bincount(e_flat, length=N_EXPERTS)
    padded = (counts + MOE_BLOCK - 1) // MOE_BLOCK * MOE_BLOCK
    pad_end = jnp.cumsum(padded)
    pad_start = pad_end - padded
    start = jnp.cumsum(counts) - counts
    dest = pad_start[e_sorted] + (jnp.arange(n_assign) - start[e_sorted])
    slot_tok = jnp.full((n_slots,), n, jnp.int32).at[dest].set(tok_flat[order])
    slot_gate = jnp.zeros((n_slots,), jnp.float32).at[dest].set(g_flat[order])
    block_expert = jnp.minimum(
        jnp.searchsorted(pad_end, jnp.arange(n_blocks) * MOE_BLOCK, side='right'), N_EXPERTS - 1)
    x_pad = jnp.concatenate([xf, jnp.zeros((1, d_), xf.dtype)], axis=0)

    def expert_block(args):
        toks, gts, e = args
        xb = x_pad[toks]
        yb = swiglu(xb, w_gate_e[e], w_up_e[e], w_down_e[e])
        return yb * gts[:, None].astype(yb.dtype)

    y_slots = lax.map(expert_block, (slot_tok.reshape(n_blocks, MOE_BLOCK),
                                     slot_gate.reshape(n_blocks, MOE_BLOCK), block_expert))
    routed = jnp.zeros((n + 1, d_), xf.dtype).at[slot_tok].add(y_slots.reshape(n_slots, d_))[:n]
    shared = swiglu(xf, w_sh_gate, w_sh_up, w_sh_down)
    return (shared + routed).reshape(b_, s_, d_)


def setup_inputs(seed: int = 0) -> dict:
    key = jax.random.key(seed)
    ks = jax.random.split(key, 26)
    L, D = DEPTH, D_MODEL

    def nrm(k, shape, scale):
        return jax.random.normal(k, shape, jnp.float32) * scale

    return {
        "x": nrm(ks[0], (BATCH, SEQ, D), 1.0),
        "c": nrm(ks[1], (BATCH, D), 1.0),
        "w_ada": nrm(ks[2], (L, D, N_MOD * D), 0.5 * D ** -0.5),
        "b_ada": nrm(ks[3], (L, N_MOD * D), 0.02),
        "norm1_w": 1.0 + nrm(ks[4], (L, D), 0.05),
        "w_in": nrm(ks[5], (L, D, IN_WIDTH), D ** -0.5),
        "pool_w": nrm(ks[6], (L, POOL_GROUPS, POOL_GROUP_DIM, POOL_GROUP_DIM), POOL_GROUP_DIM ** -0.5),
        "pool_scale": 1.0 + nrm(ks[7], (L, POOL_GROUPS, POOL_GROUP_DIM), 0.1),
        "w_branch_pool": nrm(ks[8], (L, POOL_WIDTH, D), POOL_WIDTH ** -0.5),
        "lambda_q1": nrm(ks[9], (L, HEAD_DIM), 0.1),
        "lambda_k1": nrm(ks[10], (L, HEAD_DIM), 0.1),
        "lambda_q2": nrm(ks[11], (L, HEAD_DIM), 0.1),
        "lambda_k2": nrm(ks[12], (L, HEAD_DIM), 0.1),
        "subln_w": 1.0 + nrm(ks[13], (L, V_HEAD_DIM), 0.05),
        "w_branch_attn": nrm(ks[14], (L, ATTN_WIDTH, D), ATTN_WIDTH ** -0.5),
        "w_out": nrm(ks[15], (L, D, D), D ** -0.5),
        "norm2_w": 1.0 + nrm(ks[16], (L, D), 0.05),
        "w_router": nrm(ks[17], (L, D, N_EXPERTS), D ** -0.5),
        "router_bias": nrm(ks[18], (L, N_EXPERTS), 0.01),
        "w_gate_e": nrm(ks[19], (L, N_EXPERTS, D, EXPERT_DIM), D ** -0.5),
        "w_up_e": nrm(ks[20], (L, N_EXPERTS, D, EXPERT_DIM), D ** -0.5),
        "w_down_e": nrm(ks[21], (L, N_EXPERTS, EXPERT_DIM, D), EXPERT_DIM ** -0.5),
        "w_sh_gate": nrm(ks[22], (L, D, SHARED_DIM), D ** -0.5),
        "w_sh_up": nrm(ks[23], (L, D, SHARED_DIM), D ** -0.5),
        "w_sh_down": nrm(ks[24], (L, SHARED_DIM, D), SHARED_DIM ** -0.5),
        "final_norm_w": 1.0 + nrm(ks[25], (D,), 0.05),
    }


def reference(x, c, w_ada, b_ada, norm1_w, w_in, pool_w, pool_scale, w_branch_pool,
              lambda_q1, lambda_k1, lambda_q2, lambda_k2, subln_w, w_branch_attn, w_out,
              norm2_w, w_router, router_bias, w_gate_e, w_up_e, w_down_e,
              w_sh_gate, w_sh_up, w_sh_down, final_norm_w):
    cos, sin = rotary_tables(x.shape[1])
    c_act = jax.nn.silu(c)
    for l in range(DEPTH):
        lambda_init = 0.8 - 0.6 * float(np.exp(-0.3 * l))
        mod = jnp.einsum('bd,de->be', c_act, w_ada[l]) + b_ada[l]
        sh1, sc1, g1, sh2, sc2, g2 = jnp.split(mod[:, None, :], N_MOD, axis=-1)
        h = rms_norm(x, norm1_w[l]) * (1.0 + sc1) + sh1
        x = x + g1 * hybrid_mixer(h, w_in[l], pool_w[l], pool_scale[l], w_branch_pool[l],
                                  lambda_q1[l], lambda_k1[l], lambda_q2[l], lambda_k2[l],
                                  subln_w[l], w_branch_attn[l], w_out[l], lambda_init, cos, sin)
        h = rms_norm(x, norm2_w[l]) * (1.0 + sc2) + sh2
        x = x + g2 * moe_ffn(h, w_router[l], router_bias[l], w_gate_e[l], w_up_e[l], w_down_e[l],
                             w_sh_gate[l], w_sh_up[l], w_sh_down[l])
    return rms_norm(x, final_norm_w)
```

```python
import functools

import numpy as np
import jax
import jax.numpy as jnp
from jax import lax
from jax.experimental import pallas as pl
from jax.experimental.pallas import tpu as pltpu

D = 2048
B = 8
S = 2048
N = B * S
POOL_GROUPS = 4
POOL_DIM = 256
POOL_WIDTH = POOL_GROUPS * POOL_DIM
N_HEADS = 8
HEAD_DIM = 128
V_HEAD = 2 * HEAD_DIM
QK_WIDTH = N_HEADS * 2 * HEAD_DIM
ATTN_WIDTH = N_HEADS * V_HEAD
ROT_HALF = HEAD_DIM // 8
ROPE_THETA = 500000.0
ATTN_SCALE = HEAD_DIM ** -0.5
SUBLN_EPS = 1e-5
NORM_EPS = 1e-6
LAMBDA_INIT = 0.8 - 0.6 * float(np.exp(-0.3 * 0))
IN_WIDTH = POOL_WIDTH + 2 * QK_WIDTH + ATTN_WIDTH + 2 * D
Q_OFF = POOL_WIDTH
K_OFF = Q_OFF + QK_WIDTH
V_OFF = K_OFF + QK_WIDTH
GP_OFF = V_OFF + ATTN_WIDTH
GA_OFF = GP_OFF + D
N_EXPERTS = 64
EXPERT_DIM = 512
SHARED_DIM = 512
TOP_K = 8
N_GROUPS = 8
TOPK_GROUPS = 4
GROUP_SIZE = N_EXPERTS // N_GROUPS
ROUTED_SCALE = 2.5
MOE_BLOCK = 128
N_MOD = 6
N_SLOTS = (N * TOP_K + N_EXPERTS * (MOE_BLOCK - 1) + MOE_BLOCK - 1) // MOE_BLOCK * MOE_BLOCK
N_BLOCKS = N_SLOTS // MOE_BLOCK
HALF = D // 2

LANES = 128
V7X_VMEM_LIMIT = 60000 * 1024

BF16 = jnp.bfloat16
F32 = jnp.float32
I32 = jnp.int32
U32 = jnp.uint32


def _cparams(sem, vmem_mb):
    return pltpu.CompilerParams(dimension_semantics=sem,
                                vmem_limit_bytes=min(vmem_mb * 1024 * 1024, V7X_VMEM_LIMIT))


def _pack_halves(lo, hi):
    lo_b = lax.bitcast_convert_type(lo.astype(BF16).astype(F32), U32) >> 16
    hi_b = lax.bitcast_convert_type(hi.astype(BF16).astype(F32), U32) & jnp.uint32(0xFFFF0000)
    return lo_b | hi_b


def _unpack_halves(w):
    lo = lax.bitcast_convert_type(w << 16, F32)
    hi = lax.bitcast_convert_type(w & jnp.uint32(0xFFFF0000), F32)
    return lo, hi


def _silu(x):
    return x * jax.nn.sigmoid(x)


ADA_TN = 512


def _ada_kernel(c_ref, w_ref, b_ref, o_ref):
    c = c_ref[...]
    ca = _silu(c).astype(BF16)
    o_ref[...] = jnp.dot(ca, w_ref[...].astype(BF16), preferred_element_type=F32) + b_ref[...]


def _ada(c, w_ada, b_ada):
    width = N_MOD * D
    return pl.pallas_call(
        _ada_kernel,
        out_shape=jax.ShapeDtypeStruct((B, width), F32),
        grid=(width // ADA_TN,),
        in_specs=[pl.BlockSpec((B, D), lambda j: (0, 0)),
                  pl.BlockSpec((D, ADA_TN), lambda j: (0, j)),
                  pl.BlockSpec((1, ADA_TN), lambda j: (0, j))],
        out_specs=pl.BlockSpec((B, ADA_TN), lambda j: (0, j)),
        compiler_params=_cparams(("arbitrary",), 24),
        name="ada",
    )(c, w_ada, b_ada.reshape(1, width))


NORM_TR = 512


def _norm1_kernel(x_ref, w_ref, mod_ref, h_ref):
    x = x_ref[...]
    ms = jnp.mean(x * x, axis=-1, keepdims=True)
    y = x * lax.rsqrt(ms + NORM_EPS) * w_ref[...]
    m = mod_ref[0]
    h_ref[...] = (y * (1.0 + m[1:2]) + m[0:1]).astype(BF16)


def _norm1(x2, norm_w, mod):
    per_b = S // NORM_TR
    return pl.pallas_call(
        _norm1_kernel,
        out_shape=jax.ShapeDtypeStruct((N, D), BF16),
        grid=(N // NORM_TR,),
        in_specs=[pl.BlockSpec((NORM_TR, D), lambda i: (i, 0)),
                  pl.BlockSpec((1, D), lambda i: (0, 0)),
                  pl.BlockSpec((1, N_MOD, D), lambda i: (i // per_b, 0, 0))],
        out_specs=pl.BlockSpec((NORM_TR, D), lambda i: (i, 0)),
        compiler_params=_cparams(("arbitrary",), 32),
        name="norm1",
    )(x2, norm_w.reshape(1, D), mod)


INP_TM = 1024
INP_TN = 512
ROPE_J0 = Q_OFF // INP_TN
ROPE_JK = K_OFF // INP_TN
ROPE_J1 = V_OFF // INP_TN


def _rope_tables():
    pos = np.arange(S, dtype=np.float64)
    inv_freq = 1.0 / (ROPE_THETA ** (np.arange(0, 2 * ROT_HALF, 2, dtype=np.float64) / (2 * ROT_HALF)))
    ang = pos[:, None] * inv_freq[None, :]
    cos, sin = np.cos(ang), np.sin(ang)
    t0 = np.ones((S, HEAD_DIM)); t1 = np.zeros((S, HEAD_DIM)); t2 = np.zeros((S, HEAD_DIM))
    t0[:, :ROT_HALF] = cos; t0[:, ROT_HALF:2 * ROT_HALF] = cos
    t1[:, ROT_HALF:2 * ROT_HALF] = sin
    t2[:, :ROT_HALF] = -sin
    k_tab = np.stack([t0, t1, t2])
    return jnp.asarray(np.stack([k_tab * ATTN_SCALE, k_tab]), dtype=F32)


def _inproj_kernel(h_ref, w_ref, tab_ref, o_ref):
    j = pl.program_id(1)
    acc = jnp.dot(h_ref[...], w_ref[...], preferred_element_type=F32)
    is_rope = jnp.logical_and(j >= ROPE_J0, j < ROPE_J1)

    @pl.when(is_rope)
    def _():
        t0 = tab_ref[0, 0]
        t1 = tab_ref[0, 1]
        t2 = tab_ref[0, 2]
        for s in range(INP_TN // HEAD_DIM):
            a = acc[:, s * HEAD_DIM:(s + 1) * HEAD_DIM]
            r = a * t0 + pltpu.roll(a, ROT_HALF, 1) * t1 + pltpu.roll(a, HEAD_DIM - ROT_HALF, 1) * t2
            o_ref[:, s * HEAD_DIM:(s + 1) * HEAD_DIM] = r.astype(BF16)

    @pl.when(jnp.logical_not(is_rope))
    def _():
        o_ref[...] = acc.astype(BF16)


def _inproj(h1, w_in_bf, tabs):
    per_b = S // INP_TM
    return pl.pallas_call(
        _inproj_kernel,
        out_shape=jax.ShapeDtypeStruct((N, IN_WIDTH), BF16),
        grid=(N // INP_TM, IN_WIDTH // INP_TN),
        in_specs=[pl.BlockSpec((INP_TM, D), lambda i, j: (i, 0)),
                  pl.BlockSpec((D, INP_TN), lambda i, j: (0, j)),
                  pl.BlockSpec((1, 3, INP_TM, HEAD_DIM),
                               lambda i, j: (jnp.where(j >= ROPE_JK, 1, 0), 0, i % per_b, 0))],
        out_specs=pl.BlockSpec((INP_TM, INP_TN), lambda i, j: (i, j)),
        compiler_params=_cparams(("arbitrary", "arbitrary"), 48),
        name="inproj",
    )(h1, w_in_bf, tabs)


POOL_PAD = 16


def _pool_kernel(u_ref, w_ref, sc_ref, o_ref, pad_ref):
    g = pl.program_id(1)
    u = u_ref[...].astype(F32)
    pad_ref[0:POOL_PAD, :] = jnp.zeros((POOL_PAD, POOL_DIM), F32)
    sums = []
    s = u
    for sh in (1, 2, 4, 8):
        pad_ref[POOL_PAD:POOL_PAD + S, :] = s
        s = s + pad_ref[POOL_PAD - sh:POOL_PAD - sh + S, :]
        sums.append(s)
    win_sum = jnp.where(g == 0, sums[0], jnp.where(g == 1, sums[1], jnp.where(g == 2, sums[2], sums[3])))
    win = jnp.left_shift(2, g)
    row = lax.broadcasted_iota(I32, (S, POOL_DIM), 0)
    cnt = jnp.minimum(row + 1, win).astype(F32)
    mixed = (win_sum / cnt - u).astype(BF16)
    y = jnp.dot(mixed, w_ref[0].astype(BF16), preferred_element_type=F32) * sc_ref[0]
    o_ref[...] = y.astype(BF16)


def _pool(proj, pool_w, pool_scale):
    return pl.pallas_call(
        _pool_kernel,
        out_shape=jax.ShapeDtypeStruct((N, POOL_WIDTH), BF16),
        grid=(B, POOL_GROUPS),
        in_specs=[pl.BlockSpec((S, POOL_DIM), lambda b, g: (b, g)),
                  pl.BlockSpec((1, POOL_DIM, POOL_DIM), lambda b, g: (g, 0, 0)),
                  pl.BlockSpec((1, 1, POOL_DIM), lambda b, g: (g, 0, 0))],
        out_specs=pl.BlockSpec((S, POOL_DIM), lambda b, g: (b, g)),
        scratch_shapes=[pltpu.VMEM((POOL_PAD + S, POOL_DIM), F32)],
        compiler_params=_cparams(("arbitrary", "arbitrary"), 40),
        name="pool",
    )(proj, pool_w, pool_scale.reshape(POOL_GROUPS, 1, POOL_DIM))


ATT_T = 512


def _attn_kernel(q_ref, k_ref, v_ref, lam_ref, sw_ref, o_ref, m_ref, l_ref, acc_ref):
    qi = pl.program_id(2)
    q = q_ref[...]
    qs = (q[:, :HEAD_DIM], q[:, HEAD_DIM:])
    m_ref[...] = jnp.full(m_ref.shape, -jnp.inf, F32)
    l_ref[...] = jnp.zeros(l_ref.shape, F32)
    acc_ref[...] = jnp.zeros(acc_ref.shape, F32)

    def chunk(j, masked):
        start = pl.multiple_of(j * ATT_T, ATT_T)
        ks = k_ref[pl.ds(start, ATT_T), :]
        vs = v_ref[pl.ds(start, ATT_T), :]
        for c in range(2):
            s = lax.dot_general(qs[c], ks[:, c * HEAD_DIM:(c + 1) * HEAD_DIM],
                                (((1,), (1,)), ((), ())), preferred_element_type=F32)
            if masked:
                row = lax.broadcasted_iota(I32, (ATT_T, ATT_T), 0)
                col = lax.broadcasted_iota(I32, (ATT_T, ATT_T), 1)
                s = jnp.where(col <= row, s, -jnp.inf)
            m_old = m_ref[c]
            m_new = jnp.maximum(m_old, jnp.max(s, axis=-1, keepdims=True))
            alpha = jnp.exp(m_old - m_new)
            p = jnp.exp(s - m_new)
            l_ref[c] = alpha * l_ref[c] + jnp.sum(p, axis=-1, keepdims=True)
            acc_ref[c] = alpha * acc_ref[c] + jnp.dot(p.astype(BF16), vs, preferred_element_type=F32)
            m_ref[c] = m_new

    def body(j, carry):
        chunk(j, False)
        return carry

    lax.fori_loop(0, qi, body, 0)
    chunk(qi, True)

    lp = lam_ref[...]
    lam = (jnp.exp(jnp.sum(lp[0:1] * lp[1:2], axis=-1, keepdims=True))
           - jnp.exp(jnp.sum(lp[2:3] * lp[3:4], axis=-1, keepdims=True)) + LAMBDA_INIT)
    o = acc_ref[0] / l_ref[0] - lam * (acc_ref[1] / l_ref[1])
    ms = jnp.mean(o * o, axis=-1, keepdims=True)
    o = o * lax.rsqrt(ms + SUBLN_EPS) * sw_ref[...] * (1.0 - LAMBDA_INIT)
    o_ref[...] = o.astype(BF16)


def _attn(proj, lam_params, subln_w):
    nq = S // ATT_T
    qb, kb, vb = Q_OFF // V_HEAD, K_OFF // V_HEAD, V_OFF // V_HEAD
    return pl.pallas_call(
        _attn_kernel,
        out_shape=jax.ShapeDtypeStruct((N, ATTN_WIDTH), BF16),
        grid=(B, N_HEADS, nq),
        in_specs=[pl.BlockSpec((ATT_T, V_HEAD), lambda b, h, i: (b * nq + i, qb + h)),
                  pl.BlockSpec((S, V_HEAD), lambda b, h, i: (b, kb + h)),
                  pl.BlockSpec((S, V_HEAD), lambda b, h, i: (b, vb + h)),
                  pl.BlockSpec((4, HEAD_DIM), lambda b, h, i: (0, 0)),
                  pl.BlockSpec((1, V_HEAD), lambda b, h, i: (0, 0))],
        out_specs=pl.BlockSpec((ATT_T, V_HEAD), lambda b, h, i: (b * nq + i, h)),
        scratch_shapes=[pltpu.VMEM((2, ATT_T, 1), F32),
                        pltpu.VMEM((2, ATT_T, 1), F32),
                        pltpu.VMEM((2, ATT_T, V_HEAD), F32)],
        compiler_params=_cparams(("arbitrary", "arbitrary", "arbitrary"), 40),
        name="attn",
    )(proj, proj, proj, lam_params, subln_w.reshape(1, V_HEAD))


MRG_TM = 256


def _merge_kernel(yp_ref, o_ref, gp0_ref, gp1_ref, ga0_ref, ga1_ref, x_ref, mod_ref, w2_ref,
                  wbp_ref, wba_ref, wout_ref, x1_ref, h2_ref, h2p_ref):
    a1 = jnp.dot(yp_ref[...], wbp_ref[...], preferred_element_type=F32)
    a2 = jnp.dot(o_ref[...], wba_ref[...], preferred_element_type=F32)
    m_lo = (jax.nn.sigmoid(gp0_ref[...].astype(F32)) * a1[:, :HALF]
            + jax.nn.sigmoid(ga0_ref[...].astype(F32)) * a2[:, :HALF]).astype(BF16)
    m_hi = (jax.nn.sigmoid(gp1_ref[...].astype(F32)) * a1[:, HALF:]
            + jax.nn.sigmoid(ga1_ref[...].astype(F32)) * a2[:, HALF:]).astype(BF16)
    out = (jnp.dot(m_lo, wout_ref[0:HALF, :], preferred_element_type=F32)
           + jnp.dot(m_hi, wout_ref[HALF:D, :], preferred_element_type=F32))
    m = mod_ref[0]
    x1 = x_ref[...] + m[2:3] * out
    x1_ref[...] = x1
    ms = jnp.mean(x1 * x1, axis=-1, keepdims=True)
    h2 = x1 * lax.rsqrt(ms + NORM_EPS) * w2_ref[...] * (1.0 + m[4:5]) + m[3:4]
    h2_ref[...] = h2
    h2p_ref[...] = _pack_halves(h2[:, :HALF], h2[:, HALF:])


def _merge(y_pool, o_attn, proj, x2, mod, norm2_w, wbp, wba, wout):
    per_b = S // MRG_TM
    gpb, gab = GP_OFF // HALF, GA_OFF // HALF
    row = lambda i: (i, 0)
    const = lambda i: (0, 0)
    wspec = lambda shape: pl.BlockSpec(shape, const, pipeline_mode=pl.Buffered(1))
    return pl.pallas_call(
        _merge_kernel,
        out_shape=(jax.ShapeDtypeStruct((N, D), F32),
                   jax.ShapeDtypeStruct((N, D), F32),
                   jax.ShapeDtypeStruct((N, HALF), U32)),
        grid=(N // MRG_TM,),
        in_specs=[pl.BlockSpec((MRG_TM, POOL_WIDTH), row),
                  pl.BlockSpec((MRG_TM, ATTN_WIDTH), row),
                  pl.BlockSpec((MRG_TM, HALF), lambda i: (i, gpb)),
                  pl.BlockSpec((MRG_TM, HALF), lambda i: (i, gpb + 1)),
                  pl.BlockSpec((MRG_TM, HALF), lambda i: (i, gab)),
                  pl.BlockSpec((MRG_TM, HALF), lambda i: (i, gab + 1)),
                  pl.BlockSpec((MRG_TM, D), row),
                  pl.BlockSpec((1, N_MOD, D), lambda i: (i // per_b, 0, 0)),
                  pl.BlockSpec((1, D), const),
                  wspec((POOL_WIDTH, D)),
                  wspec((ATTN_WIDTH, D)),
                  wspec((D, D))],
        out_specs=(pl.BlockSpec((MRG_TM, D), row),
                   pl.BlockSpec((MRG_TM, D), row),
                   pl.BlockSpec((MRG_TM, HALF), row)),
        compiler_params=_cparams(("arbitrary",), 58),
        name="merge",
    )(y_pool, o_attn, proj, proj, proj, proj, x2, mod, norm2_w.reshape(1, D), wbp, wba, wout)


RT_T = 512


def _router_kernel(h_ref, wr_ref, bias_ref, idx_ref, gate_ref, pos_ref, cnt_ref, carry_ref):
    i = pl.program_id(0)

    @pl.when(i == 0)
    def _():
        carry_ref[...] = jnp.zeros(carry_ref.shape, F32)

    logits = lax.dot_general(wr_ref[...], h_ref[...], (((1,), (1,)), ((), ())),
                             precision=lax.Precision.HIGHEST, preferred_element_type=F32)
    scores = jax.nn.sigmoid(logits)
    biased = scores + bias_ref[...]
    neg = -jnp.inf

    b3 = biased.reshape(N_GROUPS, GROUP_SIZE, RT_T)
    io3 = lax.broadcasted_iota(I32, b3.shape, 1)
    m1 = jnp.max(b3, axis=1, keepdims=True)
    i1 = jnp.min(jnp.where(b3 == m1, io3, GROUP_SIZE), axis=1, keepdims=True)
    m2 = jnp.max(jnp.where(io3 == i1, neg, b3), axis=1, keepdims=True)
    gs = m1 + m2

    gio = lax.broadcasted_iota(I32, gs.shape, 0)
    gsel = jnp.zeros(gs.shape, jnp.bool_)
    cur = gs
    for _ in range(TOPK_GROUPS):
        mx = jnp.max(cur, axis=0, keepdims=True)
        ix = jnp.min(jnp.where(cur == mx, gio, N_GROUPS), axis=0, keepdims=True)
        pick = gio == ix
        gsel = jnp.logical_or(gsel, pick)
        cur = jnp.where(pick, neg, cur)

    masked = jnp.where(gsel, b3, neg).reshape(N_EXPERTS, RT_T)
    eio = lax.broadcasted_iota(I32, (N_EXPERTS, RT_T), 0)
    assign = jnp.zeros((N_EXPERTS, RT_T), F32)
    idxs, sels = [], []
    for _ in range(TOP_K):
        mx = jnp.max(masked, axis=0, keepdims=True)
        ix = jnp.min(jnp.where(masked == mx, eio, N_EXPERTS), axis=0, keepdims=True)
        hot = eio == ix
        sels.append(jnp.sum(jnp.where(hot, scores, 0.0), axis=0, keepdims=True))
        idxs.append(ix)
        masked = jnp.where(hot, neg, masked)
        assign = jnp.where(hot, 1.0, assign)

    denom = sels[0]
    for k in range(1, TOP_K):
        denom = denom + sels[k]

    tp = lax.broadcasted_iota(I32, (RT_T, RT_T), 0)
    tc = lax.broadcasted_iota(I32, (RT_T, RT_T), 1)
    before = jnp.where(tp < tc, 1.0, 0.0).astype(BF16)
    rank = jnp.dot(assign.astype(BF16), before, preferred_element_type=F32) + carry_ref[...]
    for k in range(TOP_K):
        idx_ref[k:k + 1, :] = idxs[k]
        gate_ref[k:k + 1, :] = sels[k] / denom * ROUTED_SCALE
        pos_ref[k:k + 1, :] = jnp.sum(jnp.where(eio == idxs[k], rank, 0.0), axis=0,
                                      keepdims=True).astype(I32)
    carry = carry_ref[...] + jnp.sum(assign, axis=1, keepdims=True)
    carry_ref[...] = carry
    cnt_ref[...] = jnp.broadcast_to(carry, cnt_ref.shape)


def _router(h2, w_router_t, bias):
    tok = lambda i: (0, i)
    return pl.pallas_call(
        _router_kernel,
        out_shape=(jax.ShapeDtypeStruct((TOP_K, N), I32),
                   jax.ShapeDtypeStruct((TOP_K, N), F32),
                   jax.ShapeDtypeStruct((TOP_K, N), I32),
                   jax.ShapeDtypeStruct((N_EXPERTS, LANES), F32)),
        grid=(N // RT_T,),
        in_specs=[pl.BlockSpec((RT_T, D), lambda i: (i, 0)),
                  pl.BlockSpec((N_EXPERTS, D), lambda i: (0, 0)),
                  pl.BlockSpec((N_EXPERTS, 1), lambda i: (0, 0))],
        out_specs=(pl.BlockSpec((TOP_K, RT_T), tok),
                   pl.BlockSpec((TOP_K, RT_T), tok),
                   pl.BlockSpec((TOP_K, RT_T), tok),
                   pl.BlockSpec((N_EXPERTS, LANES), lambda i: (0, 0))),
        scratch_shapes=[pltpu.VMEM((N_EXPERTS, 1), F32)],
        compiler_params=_cparams(("arbitrary",), 32),
        name="router",
    )(h2, w_router_t, bias.reshape(N_EXPERTS, 1))


SLOT_T = 2048


def _slot_kernel(ps_ref, idx_ref, pos_ref, slot_ref):
    idx = idx_ref[...]
    base = jnp.zeros(idx.shape, I32)
    for e in range(N_EXPERTS):
        base = jnp.where(idx == e, ps_ref[e], base)
    slot_ref[...] = base + pos_ref[...]


def _slots(pad_start, idx, pos):
    tok = lambda i, ps: (0, i)
    return pl.pallas_call(
        _slot_kernel,
        out_shape=jax.ShapeDtypeStruct((TOP_K, N), I32),
        grid_spec=pltpu.PrefetchScalarGridSpec(
            num_scalar_prefetch=1, grid=(N // SLOT_T,),
            in_specs=[pl.BlockSpec((TOP_K, SLOT_T), tok), pl.BlockSpec((TOP_K, SLOT_T), tok)],
            out_specs=pl.BlockSpec((TOP_K, SLOT_T), tok)),
        compiler_params=_cparams(("arbitrary",), 16),
        name="slots",
    )(pad_start, idx, pos)


def _moe_kernel(be_ref, nu_ref, tokc_ref, tokn_ref, h_hbm, wg_ref, wu_ref, wd_ref, y_ref, buf, sem):
    b = pl.program_id(0)
    n_used = nu_ref[0]

    def row_copy(tok, slot, r):
        return pltpu.make_async_copy(h_hbm.at[tok], buf.at[slot, r], sem.at[slot])

    def issue(tok_ref, slot):
        for r in range(MOE_BLOCK):
            row_copy(tok_ref[0, 0, r], slot, r).start()

    @pl.when(b == 0)
    def _():
        issue(tokc_ref, 0)

    @pl.when(b + 1 < n_used)
    def _():
        issue(tokn_ref, (b + 1) % 2)

    @pl.when(b < n_used)
    def _():
        slot = b % 2
        for r in range(MOE_BLOCK):
            row_copy(0, slot, r).wait()
        lo, hi = _unpack_halves(buf[slot])
        lo = lo.astype(BF16)
        hi = hi.astype(BF16)
        g = (jnp.dot(lo, wg_ref[0, 0:HALF, :], preferred_element_type=F32)
             + jnp.dot(hi, wg_ref[0, HALF:D, :], preferred_element_type=F32))
        u = (jnp.dot(lo, wu_ref[0, 0:HALF, :], preferred_element_type=F32)
             + jnp.dot(hi, wu_ref[0, HALF:D, :], preferred_element_type=F32))
        hmid = (_silu(g) * u).astype(BF16)
        y = jnp.dot(hmid, wd_ref[0], preferred_element_type=F32)
        y_ref[...] = _pack_halves(y[:, :HALF], y[:, HALF:])

    @pl.when(b >= n_used)
    def _():
        y_ref[...] = jnp.zeros(y_ref.shape, U32)


def _moe(block_expert, n_used, slot_tok, h2p, wg, wu, wd):
    tok3 = slot_tok.reshape(N_BLOCKS, 1, MOE_BLOCK)
    smem = pltpu.SMEM
    return pl.pallas_call(
        _moe_kernel,
        out_shape=jax.ShapeDtypeStruct((N_SLOTS, HALF), U32),
        grid_spec=pltpu.PrefetchScalarGridSpec(
            num_scalar_prefetch=2, grid=(N_BLOCKS,),
            in_specs=[pl.BlockSpec((1, 1, MOE_BLOCK), lambda b, be, nu: (b, 0, 0), memory_space=smem),
                      pl.BlockSpec((1, 1, MOE_BLOCK),
                                   lambda b, be, nu: (jnp.minimum(b + 1, N_BLOCKS - 1), 0, 0),
                                   memory_space=smem),
                      pl.BlockSpec(memory_space=pl.ANY),
                      pl.BlockSpec((1, D, EXPERT_DIM), lambda b, be, nu: (be[b], 0, 0)),
                      pl.BlockSpec((1, D, EXPERT_DIM), lambda b, be, nu: (be[b], 0, 0)),
                      pl.BlockSpec((1, EXPERT_DIM, D), lambda b, be, nu: (be[b], 0, 0))],
            out_specs=pl.BlockSpec((MOE_BLOCK, HALF), lambda b, be, nu: (b, 0)),
            scratch_shapes=[pltpu.VMEM((2, MOE_BLOCK, HALF), U32),
                            pltpu.SemaphoreType.DMA((2,))]),
        compiler_params=_cparams(("arbitrary",), 40),
        name="moe",
    )(block_expert, n_used, tok3, tok3, h2p, wg, wu, wd)


FIN_T = 128


def _final_kernel(slc_ref, sln_ref, y_hbm, gt_ref, x1_ref, h2_ref, mod_ref, wsg_ref, wsu_ref, wsd_ref,
                  wf_ref, o_ref, buf, sem):
    i = pl.program_id(0)
    n_steps = pl.num_programs(0)

    def row_copy(s, slot, k, t):
        return pltpu.make_async_copy(y_hbm.at[s], buf.at[slot, k, t], sem.at[slot])

    def issue(sl_ref, slot):
        def body(t, carry):
            for k in range(TOP_K):
                row_copy(sl_ref[0, 0, t * TOP_K + k], slot, k, t).start()
            return carry
        lax.fori_loop(0, FIN_T, body, 0)

    @pl.when(i == 0)
    def _():
        issue(slc_ref, 0)

    @pl.when(i + 1 < n_steps)
    def _():
        issue(sln_ref, (i + 1) % 2)

    h = h2_ref[...].astype(BF16)
    sg = jnp.dot(h, wsg_ref[...], preferred_element_type=F32)
    su = jnp.dot(h, wsu_ref[...], preferred_element_type=F32)
    shared = jnp.dot((_silu(sg) * su).astype(BF16), wsd_ref[...], preferred_element_type=F32)

    slot = i % 2

    def wait_body(t, carry):
        for k in range(TOP_K):
            row_copy(0, slot, k, t).wait()
        return carry
    lax.fori_loop(0, FIN_T, wait_body, 0)

    gt = gt_ref[...]
    r_lo = jnp.zeros((FIN_T, HALF), F32)
    r_hi = jnp.zeros((FIN_T, HALF), F32)
    for k in range(TOP_K):
        lo, hi = _unpack_halves(buf[slot, k])
        gk = gt[:, k:k + 1]
        r_lo = r_lo + gk * lo
        r_hi = r_hi + gk * hi
    m = mod_ref[0]
    g2 = m[5:6]
    x1 = x1_ref[...]
    x2_lo = x1[:, :HALF] + g2[:, :HALF] * (shared[:, :HALF] + r_lo)
    x2_hi = x1[:, HALF:] + g2[:, HALF:] * (shared[:, HALF:] + r_hi)
    ssq = jnp.sum(x2_lo * x2_lo, axis=-1, keepdims=True) + jnp.sum(x2_hi * x2_hi, axis=-1, keepdims=True)
    inv = lax.rsqrt(ssq * (1.0 / D) + NORM_EPS)
    wf = wf_ref[...]
    o_ref[:, 0:HALF] = x2_lo * inv * wf[:, :HALF]
    o_ref[:, HALF:D] = x2_hi * inv * wf[:, HALF:]


def _final(slot_tk, y_p, gates_t, x1, h2, mod, wsg, wsu, wsd, wf):
    n_steps = N // FIN_T
    per_b = S // FIN_T
    sl3 = slot_tk.reshape(n_steps, 1, FIN_T * TOP_K)
    smem = pltpu.SMEM
    row = lambda i: (i, 0)
    const = lambda i: (0, 0)
    return pl.pallas_call(
        _final_kernel,
        out_shape=jax.ShapeDtypeStruct((N, D), F32),
        grid=(n_steps,),
        in_specs=[pl.BlockSpec((1, 1, FIN_T * TOP_K), lambda i: (i, 0, 0), memory_space=smem),
                  pl.BlockSpec((1, 1, FIN_T * TOP_K),
                               lambda i: (jnp.minimum(i + 1, n_steps - 1), 0, 0), memory_space=smem),
                  pl.BlockSpec(memory_space=pl.ANY),
                  pl.BlockSpec((FIN_T, TOP_K), row),
                  pl.BlockSpec((FIN_T, D), row),
                  pl.BlockSpec((FIN_T, D), row),
                  pl.BlockSpec((1, N_MOD, D), lambda i: (i // per_b, 0, 0)),
                  pl.BlockSpec((D, SHARED_DIM), const),
                  pl.BlockSpec((D, SHARED_DIM), const),
                  pl.BlockSpec((SHARED_DIM, D), const),
                  pl.BlockSpec((1, D), const)],
        out_specs=pl.BlockSpec((FIN_T, D), row),
        scratch_shapes=[pltpu.VMEM((2, TOP_K, FIN_T, HALF), U32),
                        pltpu.SemaphoreType.DMA((2,))],
        compiler_params=_cparams(("arbitrary",), 48),
        name="final",
    )(sl3, sl3, y_p, gates_t, x1, h2, mod, wsg, wsu, wsd, wf.reshape(1, D))


def kernel(x, c, w_ada, b_ada, norm1_w, w_in, pool_w, pool_scale, w_branch_pool, lambda_q1, lambda_k1,
           lambda_q2, lambda_k2, subln_w, w_branch_attn, w_out, norm2_w, w_router, router_bias,
           w_gate_e, w_up_e, w_down_e, w_sh_gate, w_sh_up, w_sh_down, final_norm_w):
    assert x.shape == (B, S, D) and w_ada.shape[0] == 1
    x2 = x.reshape(N, D)
    mod = _ada(c, w_ada[0], b_ada[0]).reshape(B, N_MOD, D)

    h1 = _norm1(x2, norm1_w[0], mod)
    proj = _inproj(h1, w_in[0].astype(BF16), _rope_tables())
    y_pool = _pool(proj, pool_w[0], pool_scale[0])
    lam_params = jnp.stack([lambda_q1[0], lambda_k1[0], lambda_q2[0], lambda_k2[0]])
    o_attn = _attn(proj, lam_params, subln_w[0])
    x1, h2, h2p = _merge(y_pool, o_attn, proj, x2, mod, norm2_w[0],
                         w_branch_pool[0].astype(BF16), w_branch_attn[0].astype(BF16),
                         w_out[0].astype(BF16))

    idx, gates, pos, cnt = _router(h2, w_router[0].T, router_bias[0])
    counts = cnt[:, 0].astype(I32)
    padded = (counts + MOE_BLOCK - 1) // MOE_BLOCK * MOE_BLOCK
    pad_end = jnp.cumsum(padded)
    pad_start = pad_end - padded
    n_used = (pad_end[-1:] // MOE_BLOCK).astype(I32)
    block_expert = jnp.minimum(
        jnp.searchsorted(pad_end, jnp.arange(N_BLOCKS, dtype=I32) * MOE_BLOCK, side='right'),
        N_EXPERTS - 1).astype(I32)
    slot = _slots(pad_start.astype(I32), idx, pos)
    tok_ids = jnp.broadcast_to(jnp.arange(N, dtype=I32)[None, :], (TOP_K, N))
    slot_tok = jnp.zeros((N_SLOTS,), I32).at[slot.reshape(-1)].set(tok_ids.reshape(-1), unique_indices=True)

    y_p = _moe(block_expert, n_used, slot_tok, h2p,
               w_gate_e[0].astype(BF16), w_up_e[0].astype(BF16), w_down_e[0].astype(BF16))
    out = _final(slot.T.reshape(-1), y_p, gates.T, x1, h2, mod,
                 w_sh_gate[0].astype(BF16), w_sh_up[0].astype(BF16), w_sh_down[0].astype(BF16),
                 final_norm_w)
    return out.reshape(B, S, D)
```

```python
import functools

import numpy as np
import jax
import jax.numpy as jnp
from jax import lax
from jax.experimental import pallas as pl
from jax.experimental.pallas import tpu as pltpu

D = 2048
B = 8
S = 2048
N = B * S
POOL_GROUPS = 4
POOL_DIM = 256
POOL_WIDTH = POOL_GROUPS * POOL_DIM
N_HEADS = 8
HEAD_DIM = 128
V_HEAD = 2 * HEAD_DIM
QK_WIDTH = N_HEADS * 2 * HEAD_DIM
ATTN_WIDTH = N_HEADS * V_HEAD
ROT_HALF = HEAD_DIM // 8
ROPE_THETA = 500000.0
ATTN_SCALE = HEAD_DIM ** -0.5
SUBLN_EPS = 1e-5
NORM_EPS = 1e-6
LAMBDA_INIT = 0.8 - 0.6 * float(np.exp(-0.3 * 0))
IN_WIDTH = POOL_WIDTH + 2 * QK_WIDTH + ATTN_WIDTH + 2 * D
Q_OFF = POOL_WIDTH
K_OFF = Q_OFF + QK_WIDTH
V_OFF = K_OFF + QK_WIDTH
GP_OFF = V_OFF + ATTN_WIDTH
GA_OFF = GP_OFF + D
N_EXPERTS = 64
EXPERT_DIM = 512
SHARED_DIM = 512
TOP_K = 8
N_GROUPS = 8
TOPK_GROUPS = 4
GROUP_SIZE = N_EXPERTS // N_GROUPS
ROUTED_SCALE = 2.5
MOE_BLOCK = 512
N_MOD = 6
N_SLOTS = (N * TOP_K + N_EXPERTS * (MOE_BLOCK - 1) + MOE_BLOCK - 1) // MOE_BLOCK * MOE_BLOCK
N_BLOCKS = N_SLOTS // MOE_BLOCK
HALF = D // 2

LANES = 128
V7X_VMEM_LIMIT = 60000 * 1024

BF16 = jnp.bfloat16
F32 = jnp.float32
I32 = jnp.int32
U32 = jnp.uint32


def _cparams(sem, vmem_mb):
    return pltpu.CompilerParams(dimension_semantics=sem,
                                vmem_limit_bytes=min(vmem_mb * 1024 * 1024, V7X_VMEM_LIMIT))


ROW_TILES = D // LANES


def _store_row_tiles(ref, vals):
    rows = vals.shape[0]
    for c in range(ROW_TILES):
        ref[pl.ds(c, rows, stride=ROW_TILES), :] = vals[:, c * LANES:(c + 1) * LANES]


def _load_row_tiles(ref, rows):
    return jnp.concatenate([ref[pl.ds(c, rows, stride=ROW_TILES), :] for c in range(ROW_TILES)], axis=1)


def _silu(x):
    return x * jax.nn.sigmoid(x)


ADA_TN = 512


def _ada_kernel(c_ref, w_ref, b_ref, o_ref):
    c = c_ref[...]
    ca = _silu(c).astype(BF16)
    o_ref[...] = jnp.dot(ca, w_ref[...].astype(BF16), preferred_element_type=F32) + b_ref[...]


def _ada(c, w_ada, b_ada):
    width = N_MOD * D
    return pl.pallas_call(
        _ada_kernel,
        out_shape=jax.ShapeDtypeStruct((B, width), F32),
        grid=(width // ADA_TN,),
        in_specs=[pl.BlockSpec((B, D), lambda j: (0, 0)),
                  pl.BlockSpec((D, ADA_TN), lambda j: (0, j)),
                  pl.BlockSpec((1, ADA_TN), lambda j: (0, j))],
        out_specs=pl.BlockSpec((B, ADA_TN), lambda j: (0, j)),
        compiler_params=_cparams(("arbitrary",), 24),
        name="ada",
    )(c, w_ada, b_ada.reshape(1, width))


NORM_TR = 512


def _norm1_kernel(x_ref, w_ref, mod_ref, h_ref):
    x = x_ref[...]
    ms = jnp.mean(x * x, axis=-1, keepdims=True)
    y = x * lax.rsqrt(ms + NORM_EPS) * w_ref[...]
    m = mod_ref[0]
    h_ref[...] = (y * (1.0 + m[1:2]) + m[0:1]).astype(BF16)


def _norm1(x2, norm_w, mod):
    per_b = S // NORM_TR
    return pl.pallas_call(
        _norm1_kernel,
        out_shape=jax.ShapeDtypeStruct((N, D), BF16),
        grid=(N // NORM_TR,),
        in_specs=[pl.BlockSpec((NORM_TR, D), lambda i: (i, 0)),
                  pl.BlockSpec((1, D), lambda i: (0, 0)),
                  pl.BlockSpec((1, N_MOD, D), lambda i: (i // per_b, 0, 0))],
        out_specs=pl.BlockSpec((NORM_TR, D), lambda i: (i, 0)),
        compiler_params=_cparams(("arbitrary",), 32),
        name="norm1",
    )(x2, norm_w.reshape(1, D), mod)


INP_TM = 1024
INP_TN = 512
ROPE_J0 = Q_OFF // INP_TN
ROPE_JK = K_OFF // INP_TN
ROPE_J1 = V_OFF // INP_TN


def _rope_tables():
    pos = np.arange(S, dtype=np.float64)
    inv_freq = 1.0 / (ROPE_THETA ** (np.arange(0, 2 * ROT_HALF, 2, dtype=np.float64) / (2 * ROT_HALF)))
    ang = pos[:, None] * inv_freq[None, :]
    cos, sin = np.cos(ang), np.sin(ang)
    t0 = np.ones((S, HEAD_DIM)); t1 = np.zeros((S, HEAD_DIM)); t2 = np.zeros((S, HEAD_DIM))
    t0[:, :ROT_HALF] = cos; t0[:, ROT_HALF:2 * ROT_HALF] = cos
    t1[:, ROT_HALF:2 * ROT_HALF] = sin
    t2[:, :ROT_HALF] = -sin
    k_tab = np.stack([t0, t1, t2])
    return jnp.asarray(np.stack([k_tab * ATTN_SCALE, k_tab]), dtype=F32)


def _inproj_kernel(h_ref, w_ref, tab_ref, o_ref):
    j = pl.program_id(1)
    acc = jnp.dot(h_ref[...], w_ref[...], preferred_element_type=F32)
    is_rope = jnp.logical_and(j >= ROPE_J0, j < ROPE_J1)

    @pl.when(is_rope)
    def _():
        t0 = tab_ref[0, 0]
        t1 = tab_ref[0, 1]
        t2 = tab_ref[0, 2]
        for s in range(INP_TN // HEAD_DIM):
            a = acc[:, s * HEAD_DIM:(s + 1) * HEAD_DIM]
            r = a * t0 + pltpu.roll(a, ROT_HALF, 1) * t1 + pltpu.roll(a, HEAD_DIM - ROT_HALF, 1) * t2
            o_ref[:, s * HEAD_DIM:(s + 1) * HEAD_DIM] = r.astype(BF16)

    @pl.when(jnp.logical_not(is_rope))
    def _():
        o_ref[...] = acc.astype(BF16)


def _inproj(h1, w_in_bf, tabs):
    per_b = S // INP_TM
    return pl.pallas_call(
        _inproj_kernel,
        out_shape=jax.ShapeDtypeStruct((N, IN_WIDTH), BF16),
        grid=(N // INP_TM, IN_WIDTH // INP_TN),
        in_specs=[pl.BlockSpec((INP_TM, D), lambda i, j: (i, 0)),
                  pl.BlockSpec((D, INP_TN), lambda i, j: (0, j)),
                  pl.BlockSpec((1, 3, INP_TM, HEAD_DIM),
                               lambda i, j: (jnp.where(j >= ROPE_JK, 1, 0), 0, i % per_b, 0))],
        out_specs=pl.BlockSpec((INP_TM, INP_TN), lambda i, j: (i, j)),
        compiler_params=_cparams(("arbitrary", "arbitrary"), 48),
        name="inproj",
    )(h1, w_in_bf, tabs)


POOL_PAD = 16


def _pool_kernel(u_ref, w_ref, sc_ref, o_ref, pad_ref):
    g = pl.program_id(1)
    u = u_ref[...].astype(F32)
    pad_ref[0:POOL_PAD, :] = jnp.zeros((POOL_PAD, POOL_DIM), F32)
    sums = []
    s = u
    for sh in (1, 2, 4, 8):
        pad_ref[POOL_PAD:POOL_PAD + S, :] = s
        s = s + pad_ref[POOL_PAD - sh:POOL_PAD - sh + S, :]
        sums.append(s)
    win_sum = jnp.where(g == 0, sums[0], jnp.where(g == 1, sums[1], jnp.where(g == 2, sums[2], sums[3])))
    win = jnp.left_shift(2, g)
    row = lax.broadcasted_iota(I32, (S, POOL_DIM), 0)
    cnt = jnp.minimum(row + 1, win).astype(F32)
    mixed = (win_sum / cnt - u).astype(BF16)
    y = jnp.dot(mixed, w_ref[0].astype(BF16), preferred_element_type=F32) * sc_ref[0]
    o_ref[...] = y.astype(BF16)


def _pool(proj, pool_w, pool_scale):
    return pl.pallas_call(
        _pool_kernel,
        out_shape=jax.ShapeDtypeStruct((N, POOL_WIDTH), BF16),
        grid=(B, POOL_GROUPS),
        in_specs=[pl.BlockSpec((S, POOL_DIM), lambda b, g: (b, g)),
                  pl.BlockSpec((1, POOL_DIM, POOL_DIM), lambda b, g: (g, 0, 0)),
                  pl.BlockSpec((1, 1, POOL_DIM), lambda b, g: (g, 0, 0))],
        out_specs=pl.BlockSpec((S, POOL_DIM), lambda b, g: (b, g)),
        scratch_shapes=[pltpu.VMEM((POOL_PAD + S, POOL_DIM), F32)],
        compiler_params=_cparams(("arbitrary", "arbitrary"), 40),
        name="pool",
    )(proj, pool_w, pool_scale.reshape(POOL_GROUPS, 1, POOL_DIM))


ATT_T = 512


def _attn_kernel(q_ref, k_ref, v_ref, lam_ref, sw_ref, o_ref, m_ref, l_ref, acc_ref):
    qi = pl.program_id(2)
    q = q_ref[...]
    qs = (q[:, :HEAD_DIM], q[:, HEAD_DIM:])
    m_ref[...] = jnp.full(m_ref.shape, -jnp.inf, F32)
    l_ref[...] = jnp.zeros(l_ref.shape, F32)
    acc_ref[...] = jnp.zeros(acc_ref.shape, F32)

    def chunk(j, masked):
        start = pl.multiple_of(j * ATT_T, ATT_T)
        ks = k_ref[pl.ds(start, ATT_T), :]
        vs = v_ref[pl.ds(start, ATT_T), :]
        for c in range(2):
            s = lax.dot_general(qs[c], ks[:, c * HEAD_DIM:(c + 1) * HEAD_DIM],
                                (((1,), (1,)), ((), ())), preferred_element_type=F32)
            if masked:
                row = lax.broadcasted_iota(I32, (ATT_T, ATT_T), 0)
                col = lax.broadcasted_iota(I32, (ATT_T, ATT_T), 1)
                s = jnp.where(col <= row, s, -jnp.inf)
            m_old = m_ref[c]
            m_new = jnp.maximum(m_old, jnp.max(s, axis=-1, keepdims=True))
            alpha = jnp.exp(m_old - m_new)
            p = jnp.exp(s - m_new)
            l_ref[c] = alpha * l_ref[c] + jnp.sum(p, axis=-1, keepdims=True)
            acc_ref[c] = alpha * acc_ref[c] + jnp.dot(p.astype(BF16), vs, preferred_element_type=F32)
            m_ref[c] = m_new

    def body(j, carry):
        chunk(j, False)
        return carry

    lax.fori_loop(0, qi, body, 0)
    chunk(qi, True)

    lp = lam_ref[...]
    lam = (jnp.exp(jnp.sum(lp[0:1] * lp[1:2], axis=-1, keepdims=True))
           - jnp.exp(jnp.sum(lp[2:3] * lp[3:4], axis=-1, keepdims=True)) + LAMBDA_INIT)
    o = acc_ref[0] / l_ref[0] - lam * (acc_ref[1] / l_ref[1])
    ms = jnp.mean(o * o, axis=-1, keepdims=True)
    o = o * lax.rsqrt(ms + SUBLN_EPS) * sw_ref[...] * (1.0 - LAMBDA_INIT)
    o_ref[...] = o.astype(BF16)


def _attn(proj, lam_params, subln_w):
    nq = S // ATT_T
    qb, kb, vb = Q_OFF // V_HEAD, K_OFF // V_HEAD, V_OFF // V_HEAD
    return pl.pallas_call(
        _attn_kernel,
        out_shape=jax.ShapeDtypeStruct((N, ATTN_WIDTH), BF16),
        grid=(B, N_HEADS, nq),
        in_specs=[pl.BlockSpec((ATT_T, V_HEAD), lambda b, h, i: (b * nq + i, qb + h)),
                  pl.BlockSpec((S, V_HEAD), lambda b, h, i: (b, kb + h)),
                  pl.BlockSpec((S, V_HEAD), lambda b, h, i: (b, vb + h)),
                  pl.BlockSpec((4, HEAD_DIM), lambda b, h, i: (0, 0)),
                  pl.BlockSpec((1, V_HEAD), lambda b, h, i: (0, 0))],
        out_specs=pl.BlockSpec((ATT_T, V_HEAD), lambda b, h, i: (b * nq + i, h)),
        scratch_shapes=[pltpu.VMEM((2, ATT_T, 1), F32),
                        pltpu.VMEM((2, ATT_T, 1), F32),
                        pltpu.VMEM((2, ATT_T, V_HEAD), F32)],
        compiler_params=_cparams(("arbitrary", "arbitrary", "arbitrary"), 40),
        name="attn",
    )(proj, proj, proj, lam_params, subln_w.reshape(1, V_HEAD))


MRG_TM = 256


def _merge_kernel(yp_ref, o_ref, gp0_ref, gp1_ref, ga0_ref, ga1_ref, x_ref, mod_ref, w2_ref,
                  wbp_ref, wba_ref, wout_ref, x1_ref, h2t_ref):
    a1 = jnp.dot(yp_ref[...], wbp_ref[...], preferred_element_type=F32)
    a2 = jnp.dot(o_ref[...], wba_ref[...], preferred_element_type=F32)
    m_lo = (jax.nn.sigmoid(gp0_ref[...].astype(F32)) * a1[:, :HALF]
            + jax.nn.sigmoid(ga0_ref[...].astype(F32)) * a2[:, :HALF]).astype(BF16)
    m_hi = (jax.nn.sigmoid(gp1_ref[...].astype(F32)) * a1[:, HALF:]
            + jax.nn.sigmoid(ga1_ref[...].astype(F32)) * a2[:, HALF:]).astype(BF16)
    out = (jnp.dot(m_lo, wout_ref[0:HALF, :], preferred_element_type=F32)
           + jnp.dot(m_hi, wout_ref[HALF:D, :], preferred_element_type=F32))
    m = mod_ref[0]
    x1 = x_ref[...] + m[2:3] * out
    x1_ref[...] = x1
    ms = jnp.mean(x1 * x1, axis=-1, keepdims=True)
    h2 = x1 * lax.rsqrt(ms + NORM_EPS) * w2_ref[...] * (1.0 + m[4:5]) + m[3:4]
    _store_row_tiles(h2t_ref, h2)


def _merge(y_pool, o_attn, proj, x2, mod, norm2_w, wbp, wba, wout):
    per_b = S // MRG_TM
    gpb, gab = GP_OFF // HALF, GA_OFF // HALF
    row = lambda i: (i, 0)
    const = lambda i: (0, 0)
    wspec = lambda shape: pl.BlockSpec(shape, const, pipeline_mode=pl.Buffered(1))
    return pl.pallas_call(
        _merge_kernel,
        out_shape=(jax.ShapeDtypeStruct((N, D), F32),
                   jax.ShapeDtypeStruct((N * ROW_TILES, LANES), F32)),
        grid=(N // MRG_TM,),
        in_specs=[pl.BlockSpec((MRG_TM, POOL_WIDTH), row),
                  pl.BlockSpec((MRG_TM, ATTN_WIDTH), row),
                  pl.BlockSpec((MRG_TM, HALF), lambda i: (i, gpb)),
                  pl.BlockSpec((MRG_TM, HALF), lambda i: (i, gpb + 1)),
                  pl.BlockSpec((MRG_TM, HALF), lambda i: (i, gab)),
                  pl.BlockSpec((MRG_TM, HALF), lambda i: (i, gab + 1)),
                  pl.BlockSpec((MRG_TM, D), row),
                  pl.BlockSpec((1, N_MOD, D), lambda i: (i // per_b, 0, 0)),
                  pl.BlockSpec((1, D), const),
                  wspec((POOL_WIDTH, D)),
                  wspec((ATTN_WIDTH, D)),
                  wspec((D, D))],
        out_specs=(pl.BlockSpec((MRG_TM, D), row),
                   pl.BlockSpec((MRG_TM * ROW_TILES, LANES), row)),
        compiler_params=_cparams(("arbitrary",), 58),
        name="merge",
    )(y_pool, o_attn, proj, proj, proj, proj, x2, mod, norm2_w.reshape(1, D), wbp, wba, wout)


RT_T = 512


def _router_kernel(h_ref, wr_ref, bias_ref, idx_ref, gate_ref, pos_ref, cnt_ref, carry_ref):
    i = pl.program_id(0)

    @pl.when(i == 0)
    def _():
        carry_ref[...] = jnp.zeros(carry_ref.shape, F32)

    logits = lax.dot_general(wr_ref[...], _load_row_tiles(h_ref, RT_T), (((1,), (1,)), ((), ())),
                             precision=lax.Precision.HIGHEST, preferred_element_type=F32)
    scores = jax.nn.sigmoid(logits)
    biased = scores + bias_ref[...]
    neg = -jnp.inf

    b3 = biased.reshape(N_GROUPS, GROUP_SIZE, RT_T)
    io3 = lax.broadcasted_iota(I32, b3.shape, 1)
    m1 = jnp.max(b3, axis=1, keepdims=True)
    i1 = jnp.min(jnp.where(b3 == m1, io3, GROUP_SIZE), axis=1, keepdims=True)
    m2 = jnp.max(jnp.where(io3 == i1, neg, b3), axis=1, keepdims=True)
    gs = m1 + m2

    gio = lax.broadcasted_iota(I32, gs.shape, 0)
    gsel = jnp.zeros(gs.shape, jnp.bool_)
    cur = gs
    for _ in range(TOPK_GROUPS):
        mx = jnp.max(cur, axis=0, keepdims=True)
        ix = jnp.min(jnp.where(cur == mx, gio, N_GROUPS), axis=0, keepdims=True)
        pick = gio == ix
        gsel = jnp.logical_or(gsel, pick)
        cur = jnp.where(pick, neg, cur)

    masked = jnp.where(gsel, b3, neg).reshape(N_EXPERTS, RT_T)
    eio = lax.broadcasted_iota(I32, (N_EXPERTS, RT_T), 0)
    assign = jnp.zeros((N_EXPERTS, RT_T), F32)
    idxs, sels = [], []
    for _ in range(TOP_K):
        mx = jnp.max(masked, axis=0, keepdims=True)
        ix = jnp.min(jnp.where(masked == mx, eio, N_EXPERTS), axis=0, keepdims=True)
        hot = eio == ix
        sels.append(jnp.sum(jnp.where(hot, scores, 0.0), axis=0, keepdims=True))
        idxs.append(ix)
        masked = jnp.where(hot, neg, masked)
        assign = jnp.where(hot, 1.0, assign)

    denom = sels[0]
    for k in range(1, TOP_K):
        denom = denom + sels[k]

    tp = lax.broadcasted_iota(I32, (RT_T, RT_T), 0)
    tc = lax.broadcasted_iota(I32, (RT_T, RT_T), 1)
    before = jnp.where(tp < tc, 1.0, 0.0).astype(BF16)
    rank = jnp.dot(assign.astype(BF16), before, preferred_element_type=F32) + carry_ref[...]
    for k in range(TOP_K):
        idx_ref[k:k + 1, :] = idxs[k]
        gate_ref[k:k + 1, :] = sels[k] / denom * ROUTED_SCALE
        pos_ref[k:k + 1, :] = jnp.sum(jnp.where(eio == idxs[k], rank, 0.0), axis=0,
                                      keepdims=True).astype(I32)
    carry = carry_ref[...] + jnp.sum(assign, axis=1, keepdims=True)
    carry_ref[...] = carry
    cnt_ref[...] = jnp.broadcast_to(carry, cnt_ref.shape)


def _router(h2, w_router_t, bias):
    tok = lambda i: (0, i)
    return pl.pallas_call(
        _router_kernel,
        out_shape=(jax.ShapeDtypeStruct((TOP_K, N), I32),
                   jax.ShapeDtypeStruct((TOP_K, N), F32),
                   jax.ShapeDtypeStruct((TOP_K, N), I32),
                   jax.ShapeDtypeStruct((N_EXPERTS, LANES), F32)),
        grid=(N // RT_T,),
        in_specs=[pl.BlockSpec((RT_T * ROW_TILES, LANES), lambda i: (i, 0)),
                  pl.BlockSpec((N_EXPERTS, D), lambda i: (0, 0)),
                  pl.BlockSpec((N_EXPERTS, 1), lambda i: (0, 0))],
        out_specs=(pl.BlockSpec((TOP_K, RT_T), tok),
                   pl.BlockSpec((TOP_K, RT_T), tok),
                   pl.BlockSpec((TOP_K, RT_T), tok),
                   pl.BlockSpec((N_EXPERTS, LANES), lambda i: (0, 0))),
        scratch_shapes=[pltpu.VMEM((N_EXPERTS, 1), F32)],
        compiler_params=_cparams(("arbitrary",), 32),
        name="router",
    )(h2, w_router_t, bias.reshape(N_EXPERTS, 1))


SLOT_T = 2048


def _slot_kernel(ps_ref, idx_ref, pos_ref, slot_ref):
    idx = idx_ref[...]
    base = jnp.zeros(idx.shape, I32)
    for e in range(N_EXPERTS):
        base = jnp.where(idx == e, ps_ref[e], base)
    slot_ref[...] = base + pos_ref[...]


def _slots(pad_start, idx, pos):
    tok = lambda i, ps: (0, i)
    return pl.pallas_call(
        _slot_kernel,
        out_shape=jax.ShapeDtypeStruct((TOP_K, N), I32),
        grid_spec=pltpu.PrefetchScalarGridSpec(
            num_scalar_prefetch=1, grid=(N // SLOT_T,),
            in_specs=[pl.BlockSpec((TOP_K, SLOT_T), tok), pl.BlockSpec((TOP_K, SLOT_T), tok)],
            out_specs=pl.BlockSpec((TOP_K, SLOT_T), tok)),
        compiler_params=_cparams(("arbitrary",), 16),
        name="slots",
    )(pad_start, idx, pos)


MOE_UNROLL = 8


def _moe_kernel(be_ref, nu_ref, tokc_ref, tokn_ref, h_hbm, wg_ref, wu_ref, wd_ref, y_ref,
                buf, sem, wg_s, wu_s, wd_s):
    b = pl.program_id(0)
    n_used = nu_ref[0]

    def issue(tok_ref, slot):
        def body(it, carry):
            for k in range(MOE_UNROLL):
                r = it * MOE_UNROLL + k
                src = pl.multiple_of(tok_ref[0, 0, r], ROW_TILES)
                dst = pl.multiple_of(r * ROW_TILES, ROW_TILES)
                pltpu.make_async_copy(h_hbm.at[pl.ds(src, ROW_TILES), :],
                                      buf.at[slot, pl.ds(dst, ROW_TILES), :], sem.at[slot]).start()
            return carry
        lax.fori_loop(0, MOE_BLOCK // MOE_UNROLL, body, 0)

    @pl.when(b == 0)
    def _():
        issue(tokc_ref, 0)

    @pl.when(b + 1 < n_used)
    def _():
        issue(tokn_ref, (b + 1) % 2)

    new_expert = jnp.logical_or(b == 0, be_ref[b] != be_ref[jnp.maximum(b - 1, 0)])

    @pl.when(jnp.logical_and(b < n_used, new_expert))
    def _():
        wg_s[...] = wg_ref[0].astype(BF16)
        wu_s[...] = wu_ref[0].astype(BF16)
        wd_s[...] = wd_ref[0].astype(BF16)

    @pl.when(b < n_used)
    def _():
        slot = b % 2
        pltpu.make_async_copy(h_hbm.at[pl.ds(0, MOE_BLOCK * ROW_TILES), :], buf.at[slot],
                              sem.at[slot]).wait()
        xb = _load_row_tiles(buf.at[slot], MOE_BLOCK).astype(BF16)
        g = jnp.dot(xb, wg_s[...], preferred_element_type=F32)
        u = jnp.dot(xb, wu_s[...], preferred_element_type=F32)
        hmid = (_silu(g) * u).astype(BF16)
        _store_row_tiles(y_ref, jnp.dot(hmid, wd_s[...], preferred_element_type=F32))

    @pl.when(b >= n_used)
    def _():
        y_ref[...] = jnp.zeros(y_ref.shape, F32)


def _moe(block_expert, n_used, slot_tok, h2p, wg, wu, wd):
    tok3 = slot_tok.reshape(N_BLOCKS, 1, MOE_BLOCK)
    smem = pltpu.SMEM
    return pl.pallas_call(
        _moe_kernel,
        out_shape=jax.ShapeDtypeStruct((N_SLOTS * ROW_TILES, LANES), F32),
        grid_spec=pltpu.PrefetchScalarGridSpec(
            num_scalar_prefetch=2, grid=(N_BLOCKS,),
            in_specs=[pl.BlockSpec((1, 1, MOE_BLOCK), lambda b, be, nu: (b, 0, 0), memory_space=smem),
                      pl.BlockSpec((1, 1, MOE_BLOCK),
                                   lambda b, be, nu: (jnp.minimum(b + 1, N_BLOCKS - 1), 0, 0),
                                   memory_space=smem),
                      pl.BlockSpec(memory_space=pl.ANY),
                      pl.BlockSpec((1, D, EXPERT_DIM), lambda b, be, nu: (be[b], 0, 0)),
                      pl.BlockSpec((1, D, EXPERT_DIM), lambda b, be, nu: (be[b], 0, 0)),
                      pl.BlockSpec((1, EXPERT_DIM, D), lambda b, be, nu: (be[b], 0, 0))],
            out_specs=pl.BlockSpec((MOE_BLOCK * ROW_TILES, LANES), lambda b, be, nu: (b, 0)),
            scratch_shapes=[pltpu.VMEM((2, MOE_BLOCK * ROW_TILES, LANES), F32),
                            pltpu.SemaphoreType.DMA((2,)),
                            pltpu.VMEM((D, EXPERT_DIM), BF16),
                            pltpu.VMEM((D, EXPERT_DIM), BF16),
                            pltpu.VMEM((EXPERT_DIM, D), BF16)]),
        compiler_params=_cparams(("arbitrary",), 58),
        name="moe",
    )(block_expert, n_used, tok3, tok3, h2p, wg, wu, wd)


FIN_T = 128


def _final_kernel(slc_ref, sln_ref, y_hbm, gt_ref, x1_ref, h2_ref, mod_ref, wsg_ref, wsu_ref, wsd_ref,
                  wf_ref, o_ref, buf, sem):
    i = pl.program_id(0)
    n_steps = pl.num_programs(0)

    def issue(sl_ref, slot):
        def body(t, carry):
            dst = pl.multiple_of(t * ROW_TILES, ROW_TILES)
            for k in range(TOP_K):
                src = pl.multiple_of(sl_ref[0, 0, t * TOP_K + k], ROW_TILES)
                pltpu.make_async_copy(y_hbm.at[pl.ds(src, ROW_TILES), :],
                                      buf.at[slot, k, pl.ds(dst, ROW_TILES), :], sem.at[slot]).start()
            return carry
        lax.fori_loop(0, FIN_T, body, 0)

    @pl.when(i == 0)
    def _():
        issue(slc_ref, 0)

    @pl.when(i + 1 < n_steps)
    def _():
        issue(sln_ref, (i + 1) % 2)

    h = _load_row_tiles(h2_ref, FIN_T).astype(BF16)
    sg = jnp.dot(h, wsg_ref[...], preferred_element_type=F32)
    su = jnp.dot(h, wsu_ref[...], preferred_element_type=F32)
    shared = jnp.dot((_silu(sg) * su).astype(BF16), wsd_ref[...], preferred_element_type=F32)

    slot = i % 2

    for k in range(TOP_K):
        pltpu.make_async_copy(y_hbm.at[pl.ds(0, FIN_T * ROW_TILES), :], buf.at[slot, k],
                              sem.at[slot]).wait()

    gt = gt_ref[...]
    routed = jnp.zeros((FIN_T, D), F32)
    for k in range(TOP_K):
        routed = routed + gt[:, k:k + 1] * _load_row_tiles(buf.at[slot, k], FIN_T)
    m = mod_ref[0]
    x2 = x1_ref[...] + m[5:6] * (shared + routed)
    ms = jnp.mean(x2 * x2, axis=-1, keepdims=True)
    o_ref[...] = x2 * lax.rsqrt(ms + NORM_EPS) * wf_ref[...]


def _final(slot_tk, y_p, gates_t, x1, h2, mod, wsg, wsu, wsd, wf):
    n_steps = N // FIN_T
    per_b = S // FIN_T
    sl3 = slot_tk.reshape(n_steps, 1, FIN_T * TOP_K)
    smem = pltpu.SMEM
    row = lambda i: (i, 0)
    const = lambda i: (0, 0)
    return pl.pallas_call(
        _final_kernel,
        out_shape=jax.ShapeDtypeStruct((N, D), F32),
        grid=(n_steps,),
        in_specs=[pl.BlockSpec((1, 1, FIN_T * TOP_K), lambda i: (i, 0, 0), memory_space=smem),
                  pl.BlockSpec((1, 1, FIN_T * TOP_K),
                               lambda i: (jnp.minimum(i + 1, n_steps - 1), 0, 0), memory_space=smem),
                  pl.BlockSpec(memory_space=pl.ANY),
                  pl.BlockSpec((FIN_T, TOP_K), row),
                  pl.BlockSpec((FIN_T, D), row),
                  pl.BlockSpec((FIN_T * ROW_TILES, LANES), row),
                  pl.BlockSpec((1, N_MOD, D), lambda i: (i // per_b, 0, 0)),
                  pl.BlockSpec((D, SHARED_DIM), const),
                  pl.BlockSpec((D, SHARED_DIM), const),
                  pl.BlockSpec((SHARED_DIM, D), const),
                  pl.BlockSpec((1, D), const)],
        out_specs=pl.BlockSpec((FIN_T, D), row),
        scratch_shapes=[pltpu.VMEM((2, TOP_K, FIN_T * ROW_TILES, LANES), F32),
                        pltpu.SemaphoreType.DMA((2,))],
        compiler_params=_cparams(("arbitrary",), 48),
        name="final",
    )(sl3, sl3, y_p, gates_t, x1, h2, mod, wsg, wsu, wsd, wf.reshape(1, D))


def kernel(x, c, w_ada, b_ada, norm1_w, w_in, pool_w, pool_scale, w_branch_pool, lambda_q1, lambda_k1,
           lambda_q2, lambda_k2, subln_w, w_branch_attn, w_out, norm2_w, w_router, router_bias,
           w_gate_e, w_up_e, w_down_e, w_sh_gate, w_sh_up, w_sh_down, final_norm_w):
    assert x.shape == (B, S, D) and w_ada.shape[0] == 1
    x2 = x.reshape(N, D)
    mod = _ada(c, w_ada[0], b_ada[0]).reshape(B, N_MOD, D)

    h1 = _norm1(x2, norm1_w[0], mod)
    proj = _inproj(h1, w_in[0].astype(BF16), _rope_tables())
    y_pool = _pool(proj, pool_w[0], pool_scale[0])
    lam_params = jnp.stack([lambda_q1[0], lambda_k1[0], lambda_q2[0], lambda_k2[0]])
    o_attn = _attn(proj, lam_params, subln_w[0])
    x1, h2t = _merge(y_pool, o_attn, proj, x2, mod, norm2_w[0],
                     w_branch_pool[0].astype(BF16), w_branch_attn[0].astype(BF16),
                     w_out[0].astype(BF16))

    idx, gates, pos, cnt = _router(h2t, w_router[0].T, router_bias[0])
    counts = cnt[:, 0].astype(I32)
    padded = (counts + MOE_BLOCK - 1) // MOE_BLOCK * MOE_BLOCK
    pad_end = jnp.cumsum(padded)
    pad_start = pad_end - padded
    n_used = (pad_end[-1:] // MOE_BLOCK).astype(I32)
    block_row0 = jnp.arange(N_BLOCKS, dtype=I32) * MOE_BLOCK
    block_expert = jnp.minimum(jnp.sum((pad_end[None, :] <= block_row0[:, None]).astype(I32), axis=1),
                               N_EXPERTS - 1).astype(I32)
    slot = _slots(pad_start.astype(I32), idx, pos)
    tok_rows = jnp.broadcast_to(jnp.arange(N, dtype=I32)[None, :] * ROW_TILES, (TOP_K, N))
    slot_tok = jnp.zeros((N_SLOTS,), I32).at[slot.reshape(-1)].set(tok_rows.reshape(-1), unique_indices=True)

    y_t = _moe(block_expert, n_used, slot_tok, h2t, w_gate_e[0], w_up_e[0], w_down_e[0])
    out = _final((slot.T * ROW_TILES).reshape(-1), y_t, gates.T, x1, h2t, mod,
                 w_sh_gate[0].astype(BF16), w_sh_up[0].astype(BF16), w_sh_down[0].astype(BF16),
                 final_norm_w)
    return out.reshape(B, S, D)
```

```python
import functools

import numpy as np
import jax
import jax.numpy as jnp
from jax import lax
from jax.experimental import pallas as pl
from jax.experimental.pallas import tpu as pltpu

D = 2048
B = 8
S = 2048
N = B * S
POOL_GROUPS = 4
POOL_DIM = 256
POOL_WIDTH = POOL_GROUPS * POOL_DIM
N_HEADS = 8
HEAD_DIM = 128
V_HEAD = 2 * HEAD_DIM
QK_WIDTH = N_HEADS * 2 * HEAD_DIM
ATTN_WIDTH = N_HEADS * V_HEAD
ROT_HALF = HEAD_DIM // 8
ROPE_THETA = 500000.0
ATTN_SCALE = HEAD_DIM ** -0.5
SUBLN_EPS = 1e-5
NORM_EPS = 1e-6
LAMBDA_INIT = 0.8 - 0.6 * float(np.exp(-0.3 * 0))
IN_WIDTH = POOL_WIDTH + 2 * QK_WIDTH + ATTN_WIDTH + 2 * D
Q_OFF = POOL_WIDTH
K_OFF = Q_OFF + QK_WIDTH
V_OFF = K_OFF + QK_WIDTH
GP_OFF = V_OFF + ATTN_WIDTH
GA_OFF = GP_OFF + D
N_EXPERTS = 64
EXPERT_DIM = 512
SHARED_DIM = 512
TOP_K = 8
N_GROUPS = 8
TOPK_GROUPS = 4
GROUP_SIZE = N_EXPERTS // N_GROUPS
ROUTED_SCALE = 2.5
MOE_BLOCK = 512
N_MOD = 6
N_SLOTS = (N * TOP_K + N_EXPERTS * (MOE_BLOCK - 1) + MOE_BLOCK - 1) // MOE_BLOCK * MOE_BLOCK
N_BLOCKS = N_SLOTS // MOE_BLOCK
HALF = D // 2

LANES = 128
V7X_VMEM_LIMIT = 60000 * 1024

BF16 = jnp.bfloat16
F32 = jnp.float32
I32 = jnp.int32
U32 = jnp.uint32


def _cparams(sem, vmem_mb):
    return pltpu.CompilerParams(dimension_semantics=sem,
                                vmem_limit_bytes=min(vmem_mb * 1024 * 1024, V7X_VMEM_LIMIT))


ROW_TILES = D // LANES


def _store_row_tiles(ref, vals):
    rows = vals.shape[0]
    for c in range(ROW_TILES):
        ref[pl.ds(c, rows, stride=ROW_TILES), :] = vals[:, c * LANES:(c + 1) * LANES]


def _load_row_tiles(ref, rows):
    return jnp.concatenate([ref[pl.ds(c, rows, stride=ROW_TILES), :] for c in range(ROW_TILES)], axis=1)


def _silu(x):
    return x * jax.nn.sigmoid(x)


ADA_TN = 512


def _ada_kernel(c_ref, w_ref, b_ref, o_ref):
    c = c_ref[...]
    ca = _silu(c).astype(BF16)
    o_ref[...] = jnp.dot(ca, w_ref[...].astype(BF16), preferred_element_type=F32) + b_ref[...]


def _ada(c, w_ada, b_ada):
    width = N_MOD * D
    return pl.pallas_call(
        _ada_kernel,
        out_shape=jax.ShapeDtypeStruct((B, width), F32),
        grid=(width // ADA_TN,),
        in_specs=[pl.BlockSpec((B, D), lambda j: (0, 0)),
                  pl.BlockSpec((D, ADA_TN), lambda j: (0, j)),
                  pl.BlockSpec((1, ADA_TN), lambda j: (0, j))],
        out_specs=pl.BlockSpec((B, ADA_TN), lambda j: (0, j)),
        compiler_params=_cparams(("arbitrary",), 24),
        name="ada",
    )(c, w_ada, b_ada.reshape(1, width))


NORM_TR = 512


def _norm1_kernel(x_ref, w_ref, mod_ref, h_ref):
    x = x_ref[...]
    ms = jnp.mean(x * x, axis=-1, keepdims=True)
    y = x * lax.rsqrt(ms + NORM_EPS) * w_ref[...]
    m = mod_ref[0]
    h_ref[...] = (y * (1.0 + m[1:2]) + m[0:1]).astype(BF16)


def _norm1(x2, norm_w, mod):
    per_b = S // NORM_TR
    return pl.pallas_call(
        _norm1_kernel,
        out_shape=jax.ShapeDtypeStruct((N, D), BF16),
        grid=(N // NORM_TR,),
        in_specs=[pl.BlockSpec((NORM_TR, D), lambda i: (i, 0)),
                  pl.BlockSpec((1, D), lambda i: (0, 0)),
                  pl.BlockSpec((1, N_MOD, D), lambda i: (i // per_b, 0, 0))],
        out_specs=pl.BlockSpec((NORM_TR, D), lambda i: (i, 0)),
        compiler_params=_cparams(("arbitrary",), 32),
        name="norm1",
    )(x2, norm_w.reshape(1, D), mod)


INP_TM = 1024
INP_TN = 512
ROPE_J0 = Q_OFF // INP_TN
ROPE_JK = K_OFF // INP_TN
ROPE_J1 = V_OFF // INP_TN


def _rope_tables():
    pos = np.arange(S, dtype=np.float64)
    inv_freq = 1.0 / (ROPE_THETA ** (np.arange(0, 2 * ROT_HALF, 2, dtype=np.float64) / (2 * ROT_HALF)))
    ang = pos[:, None] * inv_freq[None, :]
    cos, sin = np.cos(ang), np.sin(ang)
    t0 = np.ones((S, HEAD_DIM)); t1 = np.zeros((S, HEAD_DIM)); t2 = np.zeros((S, HEAD_DIM))
    t0[:, :ROT_HALF] = cos; t0[:, ROT_HALF:2 * ROT_HALF] = cos
    t1[:, ROT_HALF:2 * ROT_HALF] = sin
    t2[:, :ROT_HALF] = -sin
    k_tab = np.stack([t0, t1, t2])
    return jnp.asarray(np.stack([k_tab * ATTN_SCALE, k_tab]), dtype=F32)


def _inproj_kernel(h_ref, w_ref, tab_ref, o_ref):
    j = pl.program_id(1)
    acc = jnp.dot(h_ref[...], w_ref[...], preferred_element_type=F32)
    is_rope = jnp.logical_and(j >= ROPE_J0, j < ROPE_J1)

    @pl.when(is_rope)
    def _():
        t0 = tab_ref[0, 0]
        t1 = tab_ref[0, 1]
        t2 = tab_ref[0, 2]
        for s in range(INP_TN // HEAD_DIM):
            a = acc[:, s * HEAD_DIM:(s + 1) * HEAD_DIM]
            r = a * t0 + pltpu.roll(a, ROT_HALF, 1) * t1 + pltpu.roll(a, HEAD_DIM - ROT_HALF, 1) * t2
            o_ref[:, s * HEAD_DIM:(s + 1) * HEAD_DIM] = r.astype(BF16)

    @pl.when(jnp.logical_not(is_rope))
    def _():
        o_ref[...] = acc.astype(BF16)


def _inproj(h1, w_in_bf, tabs):
    per_b = S // INP_TM
    return pl.pallas_call(
        _inproj_kernel,
        out_shape=jax.ShapeDtypeStruct((N, IN_WIDTH), BF16),
        grid=(N // INP_TM, IN_WIDTH // INP_TN),
        in_specs=[pl.BlockSpec((INP_TM, D), lambda i, j: (i, 0)),
                  pl.BlockSpec((D, INP_TN), lambda i, j: (0, j)),
                  pl.BlockSpec((1, 3, INP_TM, HEAD_DIM),
                               lambda i, j: (jnp.where(j >= ROPE_JK, 1, 0), 0, i % per_b, 0))],
        out_specs=pl.BlockSpec((INP_TM, INP_TN), lambda i, j: (i, j)),
        compiler_params=_cparams(("arbitrary", "arbitrary"), 48),
        name="inproj",
    )(h1, w_in_bf, tabs)


POOL_PAD = 16


def _pool_kernel(u_ref, w_ref, sc_ref, o_ref, pad_ref):
    g = pl.program_id(1)
    u = u_ref[...].astype(F32)
    pad_ref[0:POOL_PAD, :] = jnp.zeros((POOL_PAD, POOL_DIM), F32)
    sums = []
    s = u
    for sh in (1, 2, 4, 8):
        pad_ref[POOL_PAD:POOL_PAD + S, :] = s
        s = s + pad_ref[POOL_PAD - sh:POOL_PAD - sh + S, :]
        sums.append(s)
    win_sum = jnp.where(g == 0, sums[0], jnp.where(g == 1, sums[1], jnp.where(g == 2, sums[2], sums[3])))
    win = jnp.left_shift(2, g)
    row = lax.broadcasted_iota(I32, (S, POOL_DIM), 0)
    cnt = jnp.minimum(row + 1, win).astype(F32)
    mixed = (win_sum / cnt - u).astype(BF16)
    y = jnp.dot(mixed, w_ref[0].astype(BF16), preferred_element_type=F32) * sc_ref[0]
    o_ref[...] = y.astype(BF16)


def _pool(proj, pool_w, pool_scale):
    return pl.pallas_call(
        _pool_kernel,
        out_shape=jax.ShapeDtypeStruct((N, POOL_WIDTH), BF16),
        grid=(B, POOL_GROUPS),
        in_specs=[pl.BlockSpec((S, POOL_DIM), lambda b, g: (b, g)),
                  pl.BlockSpec((1, POOL_DIM, POOL_DIM), lambda b, g: (g, 0, 0)),
                  pl.BlockSpec((1, 1, POOL_DIM), lambda b, g: (g, 0, 0))],
        out_specs=pl.BlockSpec((S, POOL_DIM), lambda b, g: (b, g)),
        scratch_shapes=[pltpu.VMEM((POOL_PAD + S, POOL_DIM), F32)],
        compiler_params=_cparams(("arbitrary", "arbitrary"), 40),
        name="pool",
    )(proj, pool_w, pool_scale.reshape(POOL_GROUPS, 1, POOL_DIM))


ATT_T = 512


def _attn_kernel(q_ref, k_ref, v_ref, lam_ref, sw_ref, o_ref, m_ref, l_ref, acc_ref):
    qi = pl.program_id(2)
    q = q_ref[...]
    qs = (q[:, :HEAD_DIM], q[:, HEAD_DIM:])
    m_ref[...] = jnp.full(m_ref.shape, -jnp.inf, F32)
    l_ref[...] = jnp.zeros(l_ref.shape, F32)
    acc_ref[...] = jnp.zeros(acc_ref.shape, F32)

    def chunk(j, masked):
        start = pl.multiple_of(j * ATT_T, ATT_T)
        ks = k_ref[pl.ds(start, ATT_T), :]
        vs = v_ref[pl.ds(start, ATT_T), :]
        for c in range(2):
            s = lax.dot_general(qs[c], ks[:, c * HEAD_DIM:(c + 1) * HEAD_DIM],
                                (((1,), (1,)), ((), ())), preferred_element_type=F32)
            if masked:
                row = lax.broadcasted_iota(I32, (ATT_T, ATT_T), 0)
                col = lax.broadcasted_iota(I32, (ATT_T, ATT_T), 1)
                s = jnp.where(col <= row, s, -jnp.inf)
            m_old = m_ref[c]
            m_new = jnp.maximum(m_old, jnp.max(s, axis=-1, keepdims=True))
            alpha = jnp.exp(m_old - m_new)
            p = jnp.exp(s - m_new)
            l_ref[c] = alpha * l_ref[c] + jnp.sum(p, axis=-1, keepdims=True)
            acc_ref[c] = alpha * acc_ref[c] + jnp.dot(p.astype(BF16), vs, preferred_element_type=F32)
            m_ref[c] = m_new

    def body(j, carry):
        chunk(j, False)
        return carry

    lax.fori_loop(0, qi, body, 0)
    chunk(qi, True)

    lp = lam_ref[...]
    lam = (jnp.exp(jnp.sum(lp[0:1] * lp[1:2], axis=-1, keepdims=True))
           - jnp.exp(jnp.sum(lp[2:3] * lp[3:4], axis=-1, keepdims=True)) + LAMBDA_INIT)
    o = acc_ref[0] / l_ref[0] - lam * (acc_ref[1] / l_ref[1])
    ms = jnp.mean(o * o, axis=-1, keepdims=True)
    o = o * lax.rsqrt(ms + SUBLN_EPS) * sw_ref[...] * (1.0 - LAMBDA_INIT)
    o_ref[...] = o.astype(BF16)


def _attn(proj, lam_params, subln_w):
    nq = S // ATT_T
    qb, kb, vb = Q_OFF // V_HEAD, K_OFF // V_HEAD, V_OFF // V_HEAD
    return pl.pallas_call(
        _attn_kernel,
        out_shape=jax.ShapeDtypeStruct((N, ATTN_WIDTH), BF16),
        grid=(B, N_HEADS, nq),
        in_specs=[pl.BlockSpec((ATT_T, V_HEAD), lambda b, h, i: (b * nq + i, qb + h)),
                  pl.BlockSpec((S, V_HEAD), lambda b, h, i: (b, kb + h)),
                  pl.BlockSpec((S, V_HEAD), lambda b, h, i: (b, vb + h)),
                  pl.BlockSpec((4, HEAD_DIM), lambda b, h, i: (0, 0)),
                  pl.BlockSpec((1, V_HEAD), lambda b, h, i: (0, 0))],
        out_specs=pl.BlockSpec((ATT_T, V_HEAD), lambda b, h, i: (b * nq + i, h)),
        scratch_shapes=[pltpu.VMEM((2, ATT_T, 1), F32),
                        pltpu.VMEM((2, ATT_T, 1), F32),
                        pltpu.VMEM((2, ATT_T, V_HEAD), F32)],
        compiler_params=_cparams(("arbitrary", "arbitrary", "arbitrary"), 40),
        name="attn",
    )(proj, proj, proj, lam_params, subln_w.reshape(1, V_HEAD))


MRG_TM = 256


def _merge_kernel(yp_ref, o_ref, gp0_ref, gp1_ref, ga0_ref, ga1_ref, x_ref, mod_ref, w2_ref,
                  wbp_ref, wba_ref, wout_ref, x1_ref, h2t_ref):
    a1 = jnp.dot(yp_ref[...], wbp_ref[...], preferred_element_type=F32)
    a2 = jnp.dot(o_ref[...], wba_ref[...], preferred_element_type=F32)
    m_lo = (jax.nn.sigmoid(gp0_ref[...].astype(F32)) * a1[:, :HALF]
            + jax.nn.sigmoid(ga0_ref[...].astype(F32)) * a2[:, :HALF]).astype(BF16)
    m_hi = (jax.nn.sigmoid(gp1_ref[...].astype(F32)) * a1[:, HALF:]
            + jax.nn.sigmoid(ga1_ref[...].astype(F32)) * a2[:, HALF:]).astype(BF16)
    out = (jnp.dot(m_lo, wout_ref[0:HALF, :], preferred_element_type=F32)
           + jnp.dot(m_hi, wout_ref[HALF:D, :], preferred_element_type=F32))
    m = mod_ref[0]
    x1 = x_ref[...] + m[2:3] * out
    x1_ref[...] = x1
    ms = jnp.mean(x1 * x1, axis=-1, keepdims=True)
    h2 = x1 * lax.rsqrt(ms + NORM_EPS) * w2_ref[...] * (1.0 + m[4:5]) + m[3:4]
    _store_row_tiles(h2t_ref, h2)


def _merge(y_pool, o_attn, proj, x2, mod, norm2_w, wbp, wba, wout):
    per_b = S // MRG_TM
    gpb, gab = GP_OFF // HALF, GA_OFF // HALF
    row = lambda i: (i, 0)
    const = lambda i: (0, 0)
    wspec = lambda shape: pl.BlockSpec(shape, const, pipeline_mode=pl.Buffered(1))
    return pl.pallas_call(
        _merge_kernel,
        out_shape=(jax.ShapeDtypeStruct((N, D), F32),
                   jax.ShapeDtypeStruct((N * ROW_TILES, LANES), F32)),
        grid=(N // MRG_TM,),
        in_specs=[pl.BlockSpec((MRG_TM, POOL_WIDTH), row),
                  pl.BlockSpec((MRG_TM, ATTN_WIDTH), row),
                  pl.BlockSpec((MRG_TM, HALF), lambda i: (i, gpb)),
                  pl.BlockSpec((MRG_TM, HALF), lambda i: (i, gpb + 1)),
                  pl.BlockSpec((MRG_TM, HALF), lambda i: (i, gab)),
                  pl.BlockSpec((MRG_TM, HALF), lambda i: (i, gab + 1)),
                  pl.BlockSpec((MRG_TM, D), row),
                  pl.BlockSpec((1, N_MOD, D), lambda i: (i // per_b, 0, 0)),
                  pl.BlockSpec((1, D), const),
                  wspec((POOL_WIDTH, D)),
                  wspec((ATTN_WIDTH, D)),
                  wspec((D, D))],
        out_specs=(pl.BlockSpec((MRG_TM, D), row),
                   pl.BlockSpec((MRG_TM * ROW_TILES, LANES), row)),
        compiler_params=_cparams(("arbitrary",), 58),
        name="merge",
    )(y_pool, o_attn, proj, proj, proj, proj, x2, mod, norm2_w.reshape(1, D), wbp, wba, wout)


RT_T = 512


def _router_kernel(h_ref, wr_ref, bias_ref, idx_ref, gate_ref, pos_ref, cnt_ref, carry_ref):
    i = pl.program_id(0)

    @pl.when(i == 0)
    def _():
        carry_ref[...] = jnp.zeros(carry_ref.shape, F32)

    logits = lax.dot_general(wr_ref[...], _load_row_tiles(h_ref, RT_T), (((1,), (1,)), ((), ())),
                             precision=lax.Precision.HIGHEST, preferred_element_type=F32)
    scores = jax.nn.sigmoid(logits)
    biased = scores + bias_ref[...]
    neg = -jnp.inf

    b3 = biased.reshape(N_GROUPS, GROUP_SIZE, RT_T)
    io3 = lax.broadcasted_iota(I32, b3.shape, 1)
    m1 = jnp.max(b3, axis=1, keepdims=True)
    i1 = jnp.min(jnp.where(b3 == m1, io3, GROUP_SIZE), axis=1, keepdims=True)
    m2 = jnp.max(jnp.where(io3 == i1, neg, b3), axis=1, keepdims=True)
    gs = m1 + m2

    gio = lax.broadcasted_iota(I32, gs.shape, 0)
    gsel = jnp.zeros(gs.shape, jnp.bool_)
    cur = gs
    for _ in range(TOPK_GROUPS):
        mx = jnp.max(cur, axis=0, keepdims=True)
        ix = jnp.min(jnp.where(cur == mx, gio, N_GROUPS), axis=0, keepdims=True)
        pick = gio == ix
        gsel = jnp.logical_or(gsel, pick)
        cur = jnp.where(pick, neg, cur)

    masked = jnp.where(gsel, b3, neg).reshape(N_EXPERTS, RT_T)
    eio = lax.broadcasted_iota(I32, (N_EXPERTS, RT_T), 0)
    assign = jnp.zeros((N_EXPERTS, RT_T), F32)
    idxs, sels = [], []
    for _ in range(TOP_K):
        mx = jnp.max(masked, axis=0, keepdims=True)
        ix = jnp.min(jnp.where(masked == mx, eio, N_EXPERTS), axis=0, keepdims=True)
        hot = eio == ix
        sels.append(jnp.sum(jnp.where(hot, scores, 0.0), axis=0, keepdims=True))
        idxs.append(ix)
        masked = jnp.where(hot, neg, masked)
        assign = jnp.where(hot, 1.0, assign)

    denom = sels[0]
    for k in range(1, TOP_K):
        denom = denom + sels[k]

    tp = lax.broadcasted_iota(I32, (RT_T, RT_T), 0)
    tc = lax.broadcasted_iota(I32, (RT_T, RT_T), 1)
    before = jnp.where(tp < tc, 1.0, 0.0).astype(BF16)
    rank = jnp.dot(assign.astype(BF16), before, preferred_element_type=F32) + carry_ref[...]
    for k in range(TOP_K):
        idx_ref[k:k + 1, :] = idxs[k]
        gate_ref[k:k + 1, :] = sels[k] / denom * ROUTED_SCALE
        pos_ref[k:k + 1, :] = jnp.sum(jnp.where(eio == idxs[k], rank, 0.0), axis=0,
                                      keepdims=True).astype(I32)
    carry = carry_ref[...] + jnp.sum(assign, axis=1, keepdims=True)
    carry_ref[...] = carry
    cnt_ref[...] = jnp.broadcast_to(carry, cnt_ref.shape)


def _router(h2, w_router_t, bias):
    tok = lambda i: (0, i)
    return pl.pallas_call(
        _router_kernel,
        out_shape=(jax.ShapeDtypeStruct((TOP_K, N), I32),
                   jax.ShapeDtypeStruct((TOP_K, N), F32),
                   jax.ShapeDtypeStruct((TOP_K, N), I32),
                   jax.ShapeDtypeStruct((N_EXPERTS, LANES), F32)),
        grid=(N // RT_T,),
        in_specs=[pl.BlockSpec((RT_T * ROW_TILES, LANES), lambda i: (i, 0)),
                  pl.BlockSpec((N_EXPERTS, D), lambda i: (0, 0)),
                  pl.BlockSpec((N_EXPERTS, 1), lambda i: (0, 0))],
        out_specs=(pl.BlockSpec((TOP_K, RT_T), tok),
                   pl.BlockSpec((TOP_K, RT_T), tok),
                   pl.BlockSpec((TOP_K, RT_T), tok),
                   pl.BlockSpec((N_EXPERTS, LANES), lambda i: (0, 0))),
        scratch_shapes=[pltpu.VMEM((N_EXPERTS, 1), F32)],
        compiler_params=_cparams(("arbitrary",), 32),
        name="router",
    )(h2, w_router_t, bias.reshape(N_EXPERTS, 1))


SLOT_T = 2048


def _slot_kernel(ps_ref, idx_ref, pos_ref, slot_ref):
    idx = idx_ref[...]
    base = jnp.zeros(idx.shape, I32)
    for e in range(N_EXPERTS):
        base = jnp.where(idx == e, ps_ref[e], base)
    slot_ref[...] = base + pos_ref[...]


def _slots(pad_start, idx, pos):
    tok = lambda i, ps: (0, i)
    return pl.pallas_call(
        _slot_kernel,
        out_shape=jax.ShapeDtypeStruct((TOP_K, N), I32),
        grid_spec=pltpu.PrefetchScalarGridSpec(
            num_scalar_prefetch=1, grid=(N // SLOT_T,),
            in_specs=[pl.BlockSpec((TOP_K, SLOT_T), tok), pl.BlockSpec((TOP_K, SLOT_T), tok)],
            out_specs=pl.BlockSpec((TOP_K, SLOT_T), tok)),
        compiler_params=_cparams(("arbitrary",), 16),
        name="slots",
    )(pad_start, idx, pos)


SUBLANES = 8
MXU_N = 256
ISSUE_GROUPS = 16


def _start_row_gather(src_hbm, off_ref, n_rows, dst_of_row, sem):
    per_group = n_rows // ISSUE_GROUPS
    state = {"next": 0}

    def start_group():
        g = state["next"]
        state["next"] = g + 1
        for r in range(g * per_group, (g + 1) * per_group):
            src = pl.multiple_of(off_ref[0, 0, r], ROW_TILES)
            pltpu.make_async_copy(src_hbm.at[pl.ds(src, ROW_TILES), :], dst_of_row(r), sem).start(
                priority=r % 2)
    return start_group


def _moe_kernel(be_ref, nu_ref, first_ref, wslot_ref, nxt_ref, tokc_ref, tokn_ref, h_hbm,
                wg_hbm, wu_hbm, wd_hbm, y_ref, buf, sem, wst_g, wst_u, wst_d, wsem, wg_s, wu_s, wd_s, xs, hs):
    b = pl.program_id(0)
    n_used = nu_ref[0]

    def weight_copies(e, ws):
        return (pltpu.make_async_copy(wg_hbm.at[e], wst_g.at[ws], wsem.at[ws, 0]),
                pltpu.make_async_copy(wu_hbm.at[e], wst_u.at[ws], wsem.at[ws, 1]),
                pltpu.make_async_copy(wd_hbm.at[e], wst_d.at[ws], wsem.at[ws, 2]))

    def gather_into(tok_ref, slot):
        return _start_row_gather(h_hbm, tok_ref, MOE_BLOCK,
                                 lambda r: buf.at[slot, r // SUBLANES, :, r % SUBLANES, :], sem.at[slot])

    def wait_gather(slot):
        pltpu.make_async_copy(buf.at[1 - slot], buf.at[slot], sem.at[slot]).wait()

    @pl.when(b == 0)
    def _():
        for cp in weight_copies(be_ref[0], 0):
            cp.start()
        start_group = gather_into(tokc_ref, 0)
        for _ in range(ISSUE_GROUPS):
            start_group()

    @pl.when(first_ref[b] == 1)
    def _():
        ws = wslot_ref[b]
        for cp in weight_copies(be_ref[b], ws):
            cp.wait()
        wg_s[...] = wst_g[ws].astype(BF16)
        wu_s[...] = wst_u[ws].astype(BF16)
        wd_s[...] = wst_d[ws].astype(BF16)
        nxt = nxt_ref[b]

        @pl.when(nxt >= 0)
        def _():
            for cp in weight_copies(nxt, 1 - ws):
                cp.start()

    @pl.when(b < n_used)
    def _():
        slot = b % 2
        wait_gather(slot)
        start_group = gather_into(tokn_ref, 1 - slot)
        for c in range(ROW_TILES):
            xs[:, c * LANES:(c + 1) * LANES] = buf[slot, :, c].reshape(MOE_BLOCK, LANES).astype(BF16)
            if c % 4 == 3:
                start_group()
        for j in range(EXPERT_DIM // MXU_N):
            cols = slice(j * MXU_N, (j + 1) * MXU_N)
            g = jnp.dot(xs[...], wg_s[:, cols], preferred_element_type=F32)
            start_group()
            u = jnp.dot(xs[...], wu_s[:, cols], preferred_element_type=F32)
            start_group()
            hs[:, cols] = (_silu(g) * u).astype(BF16)
        for n in range(D // MXU_N):
            yv = jnp.dot(hs[...], wd_s[:, n * MXU_N:(n + 1) * MXU_N], preferred_element_type=F32)
            for cc in range(MXU_N // LANES):
                c = n * (MXU_N // LANES) + cc
                y_ref[pl.ds(c, MOE_BLOCK, stride=ROW_TILES), :] = yv[:, cc * LANES:(cc + 1) * LANES]
            start_group()

    @pl.when(b == n_used)
    def _():
        wait_gather(b % 2)

    @pl.when(b >= n_used)
    def _():
        y_ref[...] = jnp.zeros(y_ref.shape, F32)


def _moe(block_expert, n_used, first, wslot, nxt, slot_tok, h2t, wg, wu, wd):
    tok3 = slot_tok.reshape(N_BLOCKS, 1, MOE_BLOCK)
    smem = pltpu.SMEM
    cur = lambda b, *_: (b, 0, 0)
    nxt_blk = lambda b, *_: (jnp.minimum(b + 1, N_BLOCKS - 1), 0, 0)
    any_spec = pl.BlockSpec(memory_space=pl.ANY)
    return pl.pallas_call(
        _moe_kernel,
        out_shape=jax.ShapeDtypeStruct((N_SLOTS * ROW_TILES, LANES), F32),
        grid_spec=pltpu.PrefetchScalarGridSpec(
            num_scalar_prefetch=5, grid=(N_BLOCKS,),
            in_specs=[pl.BlockSpec((1, 1, MOE_BLOCK), cur, memory_space=smem),
                      pl.BlockSpec((1, 1, MOE_BLOCK), nxt_blk, memory_space=smem),
                      any_spec, any_spec, any_spec, any_spec],
            out_specs=pl.BlockSpec((MOE_BLOCK * ROW_TILES, LANES), lambda b, *_: (b, 0)),
            scratch_shapes=[pltpu.VMEM((2, MOE_BLOCK // SUBLANES, ROW_TILES, SUBLANES, LANES), F32),
                            pltpu.SemaphoreType.DMA((2,)),
                            pltpu.VMEM((2, D, EXPERT_DIM), F32),
                            pltpu.VMEM((2, D, EXPERT_DIM), F32),
                            pltpu.VMEM((2, EXPERT_DIM, D), F32),
                            pltpu.SemaphoreType.DMA((2, 3)),
                            pltpu.VMEM((D, EXPERT_DIM), BF16),
                            pltpu.VMEM((D, EXPERT_DIM), BF16),
                            pltpu.VMEM((EXPERT_DIM, D), BF16),
                            pltpu.VMEM((MOE_BLOCK, D), BF16),
                            pltpu.VMEM((MOE_BLOCK, EXPERT_DIM), BF16)]),
        compiler_params=_cparams(("arbitrary",), 58),
        name="moe",
    )(block_expert, n_used, first, wslot, nxt, tok3, tok3, h2t, wg, wu, wd)


FIN_T = 128


def _final_kernel(slc_ref, sln_ref, y_hbm, gt_ref, x1_ref, h2_ref, mod_ref, wsg_ref, wsu_ref, wsd_ref,
                  wf_ref, o_ref, buf, sem, hs, x2_s):
    i = pl.program_id(0)
    n_steps = pl.num_programs(0)

    def gather_into(sl_ref, slot):
        def dst(q):
            t, k = q // TOP_K, q % TOP_K
            return buf.at[slot, k, t // SUBLANES, :, t % SUBLANES, :]
        return _start_row_gather(y_hbm, sl_ref, FIN_T * TOP_K, dst, sem.at[slot])

    def wait_gather(slot):
        pltpu.make_async_copy(buf.at[1 - slot], buf.at[slot], sem.at[slot]).wait()

    @pl.when(i == 0)
    def _():
        start_group = gather_into(slc_ref, 0)
        for _ in range(ISSUE_GROUPS):
            start_group()

    slot = i % 2
    start_group = gather_into(sln_ref, 1 - slot)

    h = _load_row_tiles(h2_ref, FIN_T).astype(BF16)
    for j in range(SHARED_DIM // MXU_N):
        cols = slice(j * MXU_N, (j + 1) * MXU_N)
        sg = jnp.dot(h, wsg_ref[:, cols], preferred_element_type=F32)
        start_group()
        su = jnp.dot(h, wsu_ref[:, cols], preferred_element_type=F32)
        start_group()
        hs[:, cols] = (_silu(sg) * su).astype(BF16)

    wait_gather(slot)
    gt = gt_ref[...]
    m = mod_ref[0]
    ssq = jnp.zeros((FIN_T, 1), F32)
    for n in range(D // MXU_N):
        cols = slice(n * MXU_N, (n + 1) * MXU_N)
        acc = jnp.dot(hs[...], wsd_ref[:, cols], preferred_element_type=F32)
        for k in range(TOP_K):
            yk = jnp.concatenate(
                [buf[slot, k, :, n * (MXU_N // LANES) + cc].reshape(FIN_T, LANES) for cc in range(MXU_N // LANES)],
                axis=1)
            acc = acc + gt[:, k:k + 1] * yk
        x2 = x1_ref[:, cols] + m[5:6, cols] * acc
        x2_s[:, cols] = x2
        ssq = ssq + jnp.sum(x2 * x2, axis=-1, keepdims=True)
        start_group()
        if n % 2 == 1:
            start_group()
    o_ref[...] = x2_s[...] * lax.rsqrt(ssq * (1.0 / D) + NORM_EPS) * wf_ref[...]

    @pl.when(i == n_steps - 1)
    def _():
        wait_gather(1 - slot)


def _final(slot_tk, y_p, gates_t, x1, h2, mod, wsg, wsu, wsd, wf):
    n_steps = N // FIN_T
    per_b = S // FIN_T
    sl3 = slot_tk.reshape(n_steps, 1, FIN_T * TOP_K)
    smem = pltpu.SMEM
    row = lambda i: (i, 0)
    const = lambda i: (0, 0)
    return pl.pallas_call(
        _final_kernel,
        out_shape=jax.ShapeDtypeStruct((N, D), F32),
        grid=(n_steps,),
        in_specs=[pl.BlockSpec((1, 1, FIN_T * TOP_K), lambda i: (i, 0, 0), memory_space=smem),
                  pl.BlockSpec((1, 1, FIN_T * TOP_K),
                               lambda i: (jnp.minimum(i + 1, n_steps - 1), 0, 0), memory_space=smem),
                  pl.BlockSpec(memory_space=pl.ANY),
                  pl.BlockSpec((FIN_T, TOP_K), row),
                  pl.BlockSpec((FIN_T, D), row),
                  pl.BlockSpec((FIN_T * ROW_TILES, LANES), row),
                  pl.BlockSpec((1, N_MOD, D), lambda i: (i // per_b, 0, 0)),
                  pl.BlockSpec((D, SHARED_DIM), const),
                  pl.BlockSpec((D, SHARED_DIM), const),
                  pl.BlockSpec((SHARED_DIM, D), const),
                  pl.BlockSpec((1, D), const)],
        out_specs=pl.BlockSpec((FIN_T, D), row),
        scratch_shapes=[pltpu.VMEM((2, TOP_K, FIN_T // SUBLANES, ROW_TILES, SUBLANES, LANES), F32),
                        pltpu.SemaphoreType.DMA((2,)),
                        pltpu.VMEM((FIN_T, SHARED_DIM), BF16),
                        pltpu.VMEM((FIN_T, D), F32)],
        compiler_params=_cparams(("arbitrary",), 48),
        name="final",
    )(sl3, sl3, y_p, gates_t, x1, h2, mod, wsg, wsu, wsd, wf.reshape(1, D))


def kernel(x, c, w_ada, b_ada, norm1_w, w_in, pool_w, pool_scale, w_branch_pool, lambda_q1, lambda_k1,
           lambda_q2, lambda_k2, subln_w, w_branch_attn, w_out, norm2_w, w_router, router_bias,
           w_gate_e, w_up_e, w_down_e, w_sh_gate, w_sh_up, w_sh_down, final_norm_w):
    assert x.shape == (B, S, D) and w_ada.shape[0] == 1
    x2 = x.reshape(N, D)
    mod = _ada(c, w_ada[0], b_ada[0]).reshape(B, N_MOD, D)

    h1 = _norm1(x2, norm1_w[0], mod)
    proj = _inproj(h1, w_in[0].astype(BF16), _rope_tables())
    y_pool = _pool(proj, pool_w[0], pool_scale[0])
    lam_params = jnp.stack([lambda_q1[0], lambda_k1[0], lambda_q2[0], lambda_k2[0]])
    o_attn = _attn(proj, lam_params, subln_w[0])
    x1, h2t = _merge(y_pool, o_attn, proj, x2, mod, norm2_w[0],
                     w_branch_pool[0].astype(BF16), w_branch_attn[0].astype(BF16),
                     w_out[0].astype(BF16))

    idx, gates, pos, cnt = _router(h2t, w_router[0].T, router_bias[0])
    counts = cnt[:, 0].astype(I32)
    padded = (counts + MOE_BLOCK - 1) // MOE_BLOCK * MOE_BLOCK
    pad_end = jnp.cumsum(padded)
    pad_start = pad_end - padded
    n_used = (pad_end[-1:] // MOE_BLOCK).astype(I32)
    block_row0 = jnp.arange(N_BLOCKS, dtype=I32) * MOE_BLOCK
    block_expert = jnp.minimum(jnp.sum((pad_end[None, :] <= block_row0[:, None]).astype(I32), axis=1),
                               N_EXPERTS - 1).astype(I32)
    slot = _slots(pad_start.astype(I32), idx, pos)
    tok_rows = jnp.broadcast_to(jnp.arange(N, dtype=I32)[None, :] * ROW_TILES, (TOP_K, N))
    slot_tok = jnp.zeros((N_SLOTS,), I32).at[slot.reshape(-1)].set(tok_rows.reshape(-1), unique_indices=True)

    bidx = jnp.arange(N_BLOCKS, dtype=I32)
    prev_expert = jnp.concatenate([jnp.full((1,), -1, I32), block_expert[:-1]])
    first = jnp.logical_and(bidx < n_used[0], block_expert != prev_expert)
    wslot = ((jnp.cumsum(first.astype(I32)) - 1) % 2).astype(I32)
    later_first = lax.cummin(jnp.where(first, bidx, N_BLOCKS)[::-1])[::-1]
    next_first = jnp.concatenate([later_first[1:], jnp.full((1,), N_BLOCKS, I32)])
    nxt = jnp.where(next_first < N_BLOCKS, block_expert[jnp.minimum(next_first, N_BLOCKS - 1)], -1).astype(I32)

    y_t = _moe(block_expert, n_used, first.astype(I32), wslot, nxt, slot_tok, h2t,
               w_gate_e[0], w_up_e[0], w_down_e[0])
    out = _final((slot.T * ROW_TILES).reshape(-1), y_t, gates.T, x1, h2t, mod,
                 w_sh_gate[0].astype(BF16), w_sh_up[0].astype(BF16), w_sh_down[0].astype(BF16),
                 final_norm_w)
    return out.reshape(B, S, D)
```

```python
import functools

import numpy as np
import jax
import jax.numpy as jnp
from jax import lax
from jax.experimental import pallas as pl
from jax.experimental.pallas import tpu as pltpu

D = 2048
B = 8
S = 2048
N = B * S
POOL_GROUPS = 4
POOL_DIM = 256
POOL_WIDTH = POOL_GROUPS * POOL_DIM
N_HEADS = 8
HEAD_DIM = 128
V_HEAD = 2 * HEAD_DIM
QK_WIDTH = N_HEADS * 2 * HEAD_DIM
ATTN_WIDTH = N_HEADS * V_HEAD
ROT_HALF = HEAD_DIM // 8
ROPE_THETA = 500000.0
ATTN_SCALE = HEAD_DIM ** -0.5
SUBLN_EPS = 1e-5
NORM_EPS = 1e-6
LAMBDA_INIT = 0.8 - 0.6 * float(np.exp(-0.3 * 0))
IN_WIDTH = POOL_WIDTH + 2 * QK_WIDTH + ATTN_WIDTH + 2 * D
Q_OFF = POOL_WIDTH
K_OFF = Q_OFF + QK_WIDTH
V_OFF = K_OFF + QK_WIDTH
GP_OFF = V_OFF + ATTN_WIDTH
GA_OFF = GP_OFF + D
N_EXPERTS = 64
EXPERT_DIM = 512
SHARED_DIM = 512
TOP_K = 8
N_GROUPS = 8
TOPK_GROUPS = 4
GROUP_SIZE = N_EXPERTS // N_GROUPS
ROUTED_SCALE = 2.5
MOE_BLOCK = 512
N_MOD = 6
N_SLOTS = (N * TOP_K + N_EXPERTS * (MOE_BLOCK - 1) + MOE_BLOCK - 1) // MOE_BLOCK * MOE_BLOCK
N_BLOCKS = N_SLOTS // MOE_BLOCK
HALF = D // 2

LANES = 128
V7X_VMEM_LIMIT = 60000 * 1024

BF16 = jnp.bfloat16
F32 = jnp.float32
I32 = jnp.int32
U32 = jnp.uint32


def _cparams(sem, vmem_mb):
    return pltpu.CompilerParams(dimension_semantics=sem,
                                vmem_limit_bytes=min(vmem_mb * 1024 * 1024, V7X_VMEM_LIMIT))


ROW_TILES = D // LANES


def _store_row_tiles(ref, vals):
    rows = vals.shape[0]
    for c in range(ROW_TILES):
        ref[pl.ds(c, rows, stride=ROW_TILES), :] = vals[:, c * LANES:(c + 1) * LANES]


def _load_row_tiles(ref, rows):
    return jnp.concatenate([ref[pl.ds(c, rows, stride=ROW_TILES), :] for c in range(ROW_TILES)], axis=1)


def _silu(x):
    return x * jax.nn.sigmoid(x)


ADA_TN = 512


def _ada_kernel(c_ref, w_ref, b_ref, o_ref):
    c = c_ref[...]
    ca = _silu(c).astype(BF16)
    o_ref[...] = jnp.dot(ca, w_ref[...].astype(BF16), preferred_element_type=F32) + b_ref[...]


def _ada(c, w_ada, b_ada):
    width = N_MOD * D
    return pl.pallas_call(
        _ada_kernel,
        out_shape=jax.ShapeDtypeStruct((B, width), F32),
        grid=(width // ADA_TN,),
        in_specs=[pl.BlockSpec((B, D), lambda j: (0, 0)),
                  pl.BlockSpec((D, ADA_TN), lambda j: (0, j)),
                  pl.BlockSpec((1, ADA_TN), lambda j: (0, j))],
        out_specs=pl.BlockSpec((B, ADA_TN), lambda j: (0, j)),
        compiler_params=_cparams(("arbitrary",), 24),
        name="ada",
    )(c, w_ada, b_ada.reshape(1, width))


NORM_TR = 512


def _norm1_kernel(x_ref, w_ref, mod_ref, h_ref):
    x = x_ref[...]
    ms = jnp.mean(x * x, axis=-1, keepdims=True)
    y = x * lax.rsqrt(ms + NORM_EPS) * w_ref[...]
    m = mod_ref[0]
    h_ref[...] = (y * (1.0 + m[1:2]) + m[0:1]).astype(BF16)


def _norm1(x2, norm_w, mod):
    per_b = S // NORM_TR
    return pl.pallas_call(
        _norm1_kernel,
        out_shape=jax.ShapeDtypeStruct((N, D), BF16),
        grid=(N // NORM_TR,),
        in_specs=[pl.BlockSpec((NORM_TR, D), lambda i: (i, 0)),
                  pl.BlockSpec((1, D), lambda i: (0, 0)),
                  pl.BlockSpec((1, N_MOD, D), lambda i: (i // per_b, 0, 0))],
        out_specs=pl.BlockSpec((NORM_TR, D), lambda i: (i, 0)),
        compiler_params=_cparams(("arbitrary",), 32),
        name="norm1",
    )(x2, norm_w.reshape(1, D), mod)


INP_TM = 1024
INP_TN = 512
ROPE_J0 = Q_OFF // INP_TN
ROPE_JK = K_OFF // INP_TN
ROPE_J1 = V_OFF // INP_TN


def _rope_tables():
    pos = np.arange(S, dtype=np.float64)
    inv_freq = 1.0 / (ROPE_THETA ** (np.arange(0, 2 * ROT_HALF, 2, dtype=np.float64) / (2 * ROT_HALF)))
    ang = pos[:, None] * inv_freq[None, :]
    cos, sin = np.cos(ang), np.sin(ang)
    t0 = np.ones((S, HEAD_DIM)); t1 = np.zeros((S, HEAD_DIM)); t2 = np.zeros((S, HEAD_DIM))
    t0[:, :ROT_HALF] = cos; t0[:, ROT_HALF:2 * ROT_HALF] = cos
    t1[:, ROT_HALF:2 * ROT_HALF] = sin
    t2[:, :ROT_HALF] = -sin
    k_tab = np.stack([t0, t1, t2])
    return jnp.asarray(np.stack([k_tab * (ATTN_SCALE * np.log2(np.e)), k_tab]), dtype=F32)


def _inproj_kernel(h_ref, w_ref, tab_ref, o_ref):
    j = pl.program_id(1)
    acc = jnp.dot(h_ref[...], w_ref[...], preferred_element_type=F32)
    is_rope = jnp.logical_and(j >= ROPE_J0, j < ROPE_J1)

    @pl.when(is_rope)
    def _():
        t0 = tab_ref[0, 0]
        t1 = tab_ref[0, 1]
        t2 = tab_ref[0, 2]
        for s in range(INP_TN // HEAD_DIM):
            a = acc[:, s * HEAD_DIM:(s + 1) * HEAD_DIM]
            r = a * t0 + pltpu.roll(a, ROT_HALF, 1) * t1 + pltpu.roll(a, HEAD_DIM - ROT_HALF, 1) * t2
            o_ref[:, s * HEAD_DIM:(s + 1) * HEAD_DIM] = r.astype(BF16)

    @pl.when(jnp.logical_not(is_rope))
    def _():
        o_ref[...] = acc.astype(BF16)


def _inproj(h1, w_in_bf, tabs):
    per_b = S // INP_TM
    return pl.pallas_call(
        _inproj_kernel,
        out_shape=jax.ShapeDtypeStruct((N, IN_WIDTH), BF16),
        grid=(N // INP_TM, IN_WIDTH // INP_TN),
        in_specs=[pl.BlockSpec((INP_TM, D), lambda i, j: (i, 0)),
                  pl.BlockSpec((D, INP_TN), lambda i, j: (0, j)),
                  pl.BlockSpec((1, 3, INP_TM, HEAD_DIM),
                               lambda i, j: (jnp.where(j >= ROPE_JK, 1, 0), 0, i % per_b, 0))],
        out_specs=pl.BlockSpec((INP_TM, INP_TN), lambda i, j: (i, j)),
        compiler_params=_cparams(("arbitrary", "arbitrary"), 48),
        name="inproj",
    )(h1, w_in_bf, tabs)


POOL_PAD = 16


def _pool_kernel(u_ref, w_ref, sc_ref, o_ref, pad_ref):
    g = pl.program_id(1)
    u = u_ref[...].astype(F32)
    pad_ref[0:POOL_PAD, :] = jnp.zeros((POOL_PAD, POOL_DIM), F32)
    sums = []
    s = u
    for sh in (1, 2, 4, 8):
        pad_ref[POOL_PAD:POOL_PAD + S, :] = s
        s = s + pad_ref[POOL_PAD - sh:POOL_PAD - sh + S, :]
        sums.append(s)
    win_sum = jnp.where(g == 0, sums[0], jnp.where(g == 1, sums[1], jnp.where(g == 2, sums[2], sums[3])))
    win = jnp.left_shift(2, g)
    row = lax.broadcasted_iota(I32, (S, POOL_DIM), 0)
    cnt = jnp.minimum(row + 1, win).astype(F32)
    mixed = (win_sum / cnt - u).astype(BF16)
    y = jnp.dot(mixed, w_ref[0].astype(BF16), preferred_element_type=F32) * sc_ref[0]
    o_ref[...] = y.astype(BF16)


def _pool(proj, pool_w, pool_scale):
    return pl.pallas_call(
        _pool_kernel,
        out_shape=jax.ShapeDtypeStruct((N, POOL_WIDTH), BF16),
        grid=(B, POOL_GROUPS),
        in_specs=[pl.BlockSpec((S, POOL_DIM), lambda b, g: (b, g)),
                  pl.BlockSpec((1, POOL_DIM, POOL_DIM), lambda b, g: (g, 0, 0)),
                  pl.BlockSpec((1, 1, POOL_DIM), lambda b, g: (g, 0, 0))],
        out_specs=pl.BlockSpec((S, POOL_DIM), lambda b, g: (b, g)),
        scratch_shapes=[pltpu.VMEM((POOL_PAD + S, POOL_DIM), F32)],
        compiler_params=_cparams(("arbitrary", "arbitrary"), 40),
        name="pool",
    )(proj, pool_w, pool_scale.reshape(POOL_GROUPS, 1, POOL_DIM))


ATT_T = 512


def _attn_kernel(q_ref, k_ref, v_ref, lam_ref, sw_ref, o_ref, s_ref, mx_ref, ls_ref, acc_ref):
    qi = pl.program_id(2)
    q = q_ref[...]
    qs = (q[:, :HEAD_DIM], q[:, HEAD_DIM:])
    mx_ref[...] = jnp.full(mx_ref.shape, -jnp.inf, F32)
    ls_ref[...] = jnp.zeros(ls_ref.shape, F32)
    acc_ref[...] = jnp.zeros(acc_ref.shape, F32)

    def scores(j, masked):
        start = pl.multiple_of(j * ATT_T, ATT_T)
        ks = k_ref[pl.ds(start, ATT_T), :]
        for c in range(2):
            s = lax.dot_general(qs[c], ks[:, c * HEAD_DIM:(c + 1) * HEAD_DIM],
                                (((1,), (1,)), ((), ())), preferred_element_type=F32)
            if masked:
                row = lax.broadcasted_iota(I32, (ATT_T, ATT_T), 0)
                col = lax.broadcasted_iota(I32, (ATT_T, ATT_T), 1)
                s = jnp.where(col <= row, s, -jnp.inf)
            s_ref[c, j] = s
            mx = mx_ref[c]
            for t in range(ATT_T // LANES):
                mx = jnp.maximum(mx, s[:, t * LANES:(t + 1) * LANES])
            mx_ref[c] = mx

    def scores_body(j, carry):
        scores(j, False)
        return carry

    lax.fori_loop(0, qi, scores_body, 0)
    scores(qi, True)

    row_max = [jnp.max(mx_ref[c], axis=-1, keepdims=True) for c in range(2)]

    def pv_body(j, carry):
        start = pl.multiple_of(j * ATT_T, ATT_T)
        vs = v_ref[pl.ds(start, ATT_T), :]
        for c in range(2):
            p = jnp.exp2(s_ref[c, j] - row_max[c])
            ls = ls_ref[c]
            for t in range(ATT_T // LANES):
                ls = ls + p[:, t * LANES:(t + 1) * LANES]
            ls_ref[c] = ls
            acc_ref[c] += jnp.dot(p.astype(BF16), vs, preferred_element_type=F32)
        return carry

    lax.fori_loop(0, qi + 1, pv_body, 0)

    lp = lam_ref[...]
    lam = (jnp.exp(jnp.sum(lp[0:1] * lp[1:2], axis=-1, keepdims=True))
           - jnp.exp(jnp.sum(lp[2:3] * lp[3:4], axis=-1, keepdims=True)) + LAMBDA_INIT)
    l1 = jnp.sum(ls_ref[0], axis=-1, keepdims=True)
    l2 = jnp.sum(ls_ref[1], axis=-1, keepdims=True)
    o = acc_ref[0] / l1 - lam * (acc_ref[1] / l2)
    ms = jnp.mean(o * o, axis=-1, keepdims=True)
    o = o * lax.rsqrt(ms + SUBLN_EPS) * sw_ref[...] * (1.0 - LAMBDA_INIT)
    o_ref[...] = o.astype(BF16)


def _attn(proj, lam_params, subln_w):
    nq = S // ATT_T
    qb, kb, vb = Q_OFF // V_HEAD, K_OFF // V_HEAD, V_OFF // V_HEAD
    return pl.pallas_call(
        _attn_kernel,
        out_shape=jax.ShapeDtypeStruct((N, ATTN_WIDTH), BF16),
        grid=(B, N_HEADS, nq),
        in_specs=[pl.BlockSpec((ATT_T, V_HEAD), lambda b, h, i: (b * nq + i, qb + h)),
                  pl.BlockSpec((S, V_HEAD), lambda b, h, i: (b, kb + h)),
                  pl.BlockSpec((S, V_HEAD), lambda b, h, i: (b, vb + h)),
                  pl.BlockSpec((4, HEAD_DIM), lambda b, h, i: (0, 0)),
                  pl.BlockSpec((1, V_HEAD), lambda b, h, i: (0, 0))],
        out_specs=pl.BlockSpec((ATT_T, V_HEAD), lambda b, h, i: (b * nq + i, h)),
        scratch_shapes=[pltpu.VMEM((2, S // ATT_T, ATT_T, ATT_T), F32),
                        pltpu.VMEM((2, ATT_T, LANES), F32),
                        pltpu.VMEM((2, ATT_T, LANES), F32),
                        pltpu.VMEM((2, ATT_T, V_HEAD), F32)],
        compiler_params=_cparams(("arbitrary", "arbitrary", "arbitrary"), 40),
        name="attn",
    )(proj, proj, proj, lam_params, subln_w.reshape(1, V_HEAD))


MRG_TM = 256


def _merge_kernel(yp_ref, o_ref, gp0_ref, gp1_ref, ga0_ref, ga1_ref, x_ref, mod_ref, w2_ref,
                  wbp_ref, wba_ref, wout_ref, x1_ref, h2t_ref):
    a1 = jnp.dot(yp_ref[...], wbp_ref[...], preferred_element_type=F32)
    a2 = jnp.dot(o_ref[...], wba_ref[...], preferred_element_type=F32)
    m_lo = (jax.nn.sigmoid(gp0_ref[...].astype(F32)) * a1[:, :HALF]
            + jax.nn.sigmoid(ga0_ref[...].astype(F32)) * a2[:, :HALF]).astype(BF16)
    m_hi = (jax.nn.sigmoid(gp1_ref[...].astype(F32)) * a1[:, HALF:]
            + jax.nn.sigmoid(ga1_ref[...].astype(F32)) * a2[:, HALF:]).astype(BF16)
    out = (jnp.dot(m_lo, wout_ref[0:HALF, :], preferred_element_type=F32)
           + jnp.dot(m_hi, wout_ref[HALF:D, :], preferred_element_type=F32))
    m = mod_ref[0]
    x1 = x_ref[...] + m[2:3] * out
    x1_ref[...] = x1
    ms = jnp.mean(x1 * x1, axis=-1, keepdims=True)
    h2 = x1 * lax.rsqrt(ms + NORM_EPS) * w2_ref[...] * (1.0 + m[4:5]) + m[3:4]
    _store_row_tiles(h2t_ref, h2)


def _merge(y_pool, o_attn, proj, x2, mod, norm2_w, wbp, wba, wout):
    per_b = S // MRG_TM
    gpb, gab = GP_OFF // HALF, GA_OFF // HALF
    row = lambda i: (i, 0)
    const = lambda i: (0, 0)
    wspec = lambda shape: pl.BlockSpec(shape, const, pipeline_mode=pl.Buffered(1))
    return pl.pallas_call(
        _merge_kernel,
        out_shape=(jax.ShapeDtypeStruct((N, D), F32),
                   jax.ShapeDtypeStruct((N * ROW_TILES, LANES), F32)),
        grid=(N // MRG_TM,),
        in_specs=[pl.BlockSpec((MRG_TM, POOL_WIDTH), row),
                  pl.BlockSpec((MRG_TM, ATTN_WIDTH), row),
                  pl.BlockSpec((MRG_TM, HALF), lambda i: (i, gpb)),
                  pl.BlockSpec((MRG_TM, HALF), lambda i: (i, gpb + 1)),
                  pl.BlockSpec((MRG_TM, HALF), lambda i: (i, gab)),
                  pl.BlockSpec((MRG_TM, HALF), lambda i: (i, gab + 1)),
                  pl.BlockSpec((MRG_TM, D), row),
                  pl.BlockSpec((1, N_MOD, D), lambda i: (i // per_b, 0, 0)),
                  pl.BlockSpec((1, D), const),
                  wspec((POOL_WIDTH, D)),
                  wspec((ATTN_WIDTH, D)),
                  wspec((D, D))],
        out_specs=(pl.BlockSpec((MRG_TM, D), row),
                   pl.BlockSpec((MRG_TM * ROW_TILES, LANES), row)),
        compiler_params=_cparams(("arbitrary",), 58),
        name="merge",
    )(y_pool, o_attn, proj, proj, proj, proj, x2, mod, norm2_w.reshape(1, D), wbp, wba, wout)


RT_T = 512


def _router_kernel(h_ref, wr_ref, bias_ref, idx_ref, gate_ref, pos_ref, cnt_ref, carry_ref):
    i = pl.program_id(0)

    @pl.when(i == 0)
    def _():
        carry_ref[...] = jnp.zeros(carry_ref.shape, F32)

    logits = lax.dot_general(wr_ref[...], _load_row_tiles(h_ref, RT_T), (((1,), (1,)), ((), ())),
                             precision=lax.Precision.HIGHEST, preferred_element_type=F32)
    scores = jax.nn.sigmoid(logits)
    biased = scores + bias_ref[...]
    neg = -jnp.inf

    b3 = biased.reshape(N_GROUPS, GROUP_SIZE, RT_T)
    io3 = lax.broadcasted_iota(I32, b3.shape, 1)
    m1 = jnp.max(b3, axis=1, keepdims=True)
    i1 = jnp.min(jnp.where(b3 == m1, io3, GROUP_SIZE), axis=1, keepdims=True)
    m2 = jnp.max(jnp.where(io3 == i1, neg, b3), axis=1, keepdims=True)
    gs = m1 + m2

    gio = lax.broadcasted_iota(I32, gs.shape, 0)
    gsel = jnp.zeros(gs.shape, jnp.bool_)
    cur = gs
    for _ in range(TOPK_GROUPS):
        mx = jnp.max(cur, axis=0, keepdims=True)
        ix = jnp.min(jnp.where(cur == mx, gio, N_GROUPS), axis=0, keepdims=True)
        pick = gio == ix
        gsel = jnp.logical_or(gsel, pick)
        cur = jnp.where(pick, neg, cur)

    masked = jnp.where(gsel, b3, neg).reshape(N_EXPERTS, RT_T)
    eio = lax.broadcasted_iota(I32, (N_EXPERTS, RT_T), 0)
    assign = jnp.zeros((N_EXPERTS, RT_T), F32)
    idxs, sels = [], []
    for _ in range(TOP_K):
        mx = jnp.max(masked, axis=0, keepdims=True)
        ix = jnp.min(jnp.where(masked == mx, eio, N_EXPERTS), axis=0, keepdims=True)
        hot = eio == ix
        sels.append(jnp.sum(jnp.where(hot, scores, 0.0), axis=0, keepdims=True))
        idxs.append(ix)
        masked = jnp.where(hot, neg, masked)
        assign = jnp.where(hot, 1.0, assign)

    denom = sels[0]
    for k in range(1, TOP_K):
        denom = denom + sels[k]

    tp = lax.broadcasted_iota(I32, (RT_T, RT_T), 0)
    tc = lax.broadcasted_iota(I32, (RT_T, RT_T), 1)
    before = jnp.where(tp < tc, 1.0, 0.0).astype(BF16)
    rank = jnp.dot(assign.astype(BF16), before, preferred_element_type=F32) + carry_ref[...]
    for k in range(TOP_K):
        idx_ref[k:k + 1, :] = idxs[k]
        gate_ref[k:k + 1, :] = sels[k] / denom * ROUTED_SCALE
        pos_ref[k:k + 1, :] = jnp.sum(jnp.where(eio == idxs[k], rank, 0.0), axis=0,
                                      keepdims=True).astype(I32)
    carry = carry_ref[...] + jnp.sum(assign, axis=1, keepdims=True)
    carry_ref[...] = carry
    cnt_ref[...] = jnp.broadcast_to(carry, cnt_ref.shape)


def _router(h2, w_router_t, bias):
    tok = lambda i: (0, i)
    return pl.pallas_call(
        _router_kernel,
        out_shape=(jax.ShapeDtypeStruct((TOP_K, N), I32),
                   jax.ShapeDtypeStruct((TOP_K, N), F32),
                   jax.ShapeDtypeStruct((TOP_K, N), I32),
                   jax.ShapeDtypeStruct((N_EXPERTS, LANES), F32)),
        grid=(N // RT_T,),
        in_specs=[pl.BlockSpec((RT_T * ROW_TILES, LANES), lambda i: (i, 0)),
                  pl.BlockSpec((N_EXPERTS, D), lambda i: (0, 0)),
                  pl.BlockSpec((N_EXPERTS, 1), lambda i: (0, 0))],
        out_specs=(pl.BlockSpec((TOP_K, RT_T), tok),
                   pl.BlockSpec((TOP_K, RT_T), tok),
                   pl.BlockSpec((TOP_K, RT_T), tok),
                   pl.BlockSpec((N_EXPERTS, LANES), lambda i: (0, 0))),
        scratch_shapes=[pltpu.VMEM((N_EXPERTS, 1), F32)],
        compiler_params=_cparams(("arbitrary",), 32),
        name="router",
    )(h2, w_router_t, bias.reshape(N_EXPERTS, 1))


SLOT_T = 2048


def _slot_kernel(ps_ref, idx_ref, pos_ref, slot_ref):
    idx = idx_ref[...]
    base = jnp.zeros(idx.shape, I32)
    for e in range(N_EXPERTS):
        base = jnp.where(idx == e, ps_ref[e], base)
    slot_ref[...] = base + pos_ref[...]


def _slots(pad_start, idx, pos):
    tok = lambda i, ps: (0, i)
    return pl.pallas_call(
        _slot_kernel,
        out_shape=jax.ShapeDtypeStruct((TOP_K, N), I32),
        grid_spec=pltpu.PrefetchScalarGridSpec(
            num_scalar_prefetch=1, grid=(N // SLOT_T,),
            in_specs=[pl.BlockSpec((TOP_K, SLOT_T), tok), pl.BlockSpec((TOP_K, SLOT_T), tok)],
            out_specs=pl.BlockSpec((TOP_K, SLOT_T), tok)),
        compiler_params=_cparams(("arbitrary",), 16),
        name="slots",
    )(pad_start, idx, pos)


SUBLANES = 8
MXU_N = 256
ISSUE_GROUPS = 16


def _start_row_gather(src_hbm, off_ref, n_rows, dst_of_row, sem):
    per_group = n_rows // ISSUE_GROUPS
    state = {"next": 0}

    def start_group():
        g = state["next"]
        state["next"] = g + 1
        for r in range(g * per_group, (g + 1) * per_group):
            src = pl.multiple_of(off_ref[0, 0, r], ROW_TILES)
            pltpu.make_async_copy(src_hbm.at[pl.ds(src, ROW_TILES), :], dst_of_row(r), sem).start(
                priority=r % 2)
    return start_group


def _moe_kernel(be_ref, nu_ref, first_ref, wslot_ref, nxt_ref, tokc_ref, tokn_ref, h_hbm,
                wg_hbm, wu_hbm, wd_hbm, y_ref, buf, sem, wst_g, wst_u, wst_d, wsem, wg_s, wu_s, wd_s, xs, hs):
    b = pl.program_id(0)
    n_used = nu_ref[0]

    def weight_copies(e, ws):
        return (pltpu.make_async_copy(wg_hbm.at[e], wst_g.at[ws], wsem.at[ws, 0]),
                pltpu.make_async_copy(wu_hbm.at[e], wst_u.at[ws], wsem.at[ws, 1]),
                pltpu.make_async_copy(wd_hbm.at[e], wst_d.at[ws], wsem.at[ws, 2]))

    def gather_into(tok_ref, slot):
        return _start_row_gather(h_hbm, tok_ref, MOE_BLOCK,
                                 lambda r: buf.at[slot, r // SUBLANES, :, r % SUBLANES, :], sem.at[slot])

    def wait_gather(slot):
        pltpu.make_async_copy(buf.at[1 - slot], buf.at[slot], sem.at[slot]).wait()

    @pl.when(b == 0)
    def _():
        for cp in weight_copies(be_ref[0], 0):
            cp.start()
        start_group = gather_into(tokc_ref, 0)
        for _ in range(ISSUE_GROUPS):
            start_group()

    @pl.when(first_ref[b] == 1)
    def _():
        ws = wslot_ref[b]
        for cp in weight_copies(be_ref[b], ws):
            cp.wait()
        wg_s[...] = wst_g[ws].astype(BF16)
        wu_s[...] = wst_u[ws].astype(BF16)
        wd_s[...] = wst_d[ws].astype(BF16)
        nxt = nxt_ref[b]

        @pl.when(nxt >= 0)
        def _():
            for cp in weight_copies(nxt, 1 - ws):
                cp.start()

    @pl.when(b < n_used)
    def _():
        slot = b % 2
        wait_gather(slot)
        start_group = gather_into(tokn_ref, 1 - slot)
        for c in range(ROW_TILES):
            xs[:, c * LANES:(c + 1) * LANES] = buf[slot, :, c].reshape(MOE_BLOCK, LANES).astype(BF16)
            if c % 4 == 3:
                start_group()
        for j in range(EXPERT_DIM // MXU_N):
            cols = slice(j * MXU_N, (j + 1) * MXU_N)
            g = jnp.dot(xs[...], wg_s[:, cols], preferred_element_type=F32)
            start_group()
            u = jnp.dot(xs[...], wu_s[:, cols], preferred_element_type=F32)
            start_group()
            hs[:, cols] = (_silu(g) * u).astype(BF16)
        for n in range(D // MXU_N):
            yv = jnp.dot(hs[...], wd_s[:, n * MXU_N:(n + 1) * MXU_N], preferred_element_type=F32)
            for cc in range(MXU_N // LANES):
                c = n * (MXU_N // LANES) + cc
                y_ref[pl.ds(c, MOE_BLOCK, stride=ROW_TILES), :] = yv[:, cc * LANES:(cc + 1) * LANES]
            start_group()

    @pl.when(b == n_used)
    def _():
        wait_gather(b % 2)

    @pl.when(b >= n_used)
    def _():
        y_ref[...] = jnp.zeros(y_ref.shape, F32)


def _moe(block_expert, n_used, first, wslot, nxt, slot_tok, h2t, wg, wu, wd):
    tok3 = slot_tok.reshape(N_BLOCKS, 1, MOE_BLOCK)
    smem = pltpu.SMEM
    cur = lambda b, *_: (b, 0, 0)
    nxt_blk = lambda b, *_: (jnp.minimum(b + 1, N_BLOCKS - 1), 0, 0)
    any_spec = pl.BlockSpec(memory_space=pl.ANY)
    return pl.pallas_call(
        _moe_kernel,
        out_shape=jax.ShapeDtypeStruct((N_SLOTS * ROW_TILES, LANES), F32),
        grid_spec=pltpu.PrefetchScalarGridSpec(
            num_scalar_prefetch=5, grid=(N_BLOCKS,),
            in_specs=[pl.BlockSpec((1, 1, MOE_BLOCK), cur, memory_space=smem),
                      pl.BlockSpec((1, 1, MOE_BLOCK), nxt_blk, memory_space=smem),
                      any_spec, any_spec, any_spec, any_spec],
            out_specs=pl.BlockSpec((MOE_BLOCK * ROW_TILES, LANES), lambda b, *_: (b, 0)),
            scratch_shapes=[pltpu.VMEM((2, MOE_BLOCK // SUBLANES, ROW_TILES, SUBLANES, LANES), F32),
                            pltpu.SemaphoreType.DMA((2,)),
                            pltpu.VMEM((2, D, EXPERT_DIM), F32),
                            pltpu.VMEM((2, D, EXPERT_DIM), F32),
                            pltpu.VMEM((2, EXPERT_DIM, D), F32),
                            pltpu.SemaphoreType.DMA((2, 3)),
                            pltpu.VMEM((D, EXPERT_DIM), BF16),
                            pltpu.VMEM((D, EXPERT_DIM), BF16),
                            pltpu.VMEM((EXPERT_DIM, D), BF16),
                            pltpu.VMEM((MOE_BLOCK, D), BF16),
                            pltpu.VMEM((MOE_BLOCK, EXPERT_DIM), BF16)]),
        compiler_params=_cparams(("arbitrary",), 58),
        name="moe",
    )(block_expert, n_used, first, wslot, nxt, tok3, tok3, h2t, wg, wu, wd)


FIN_T = 128


def _final_kernel(slc_ref, sln_ref, y_hbm, gt_ref, x1_ref, h2_ref, mod_ref, wsg_ref, wsu_ref, wsd_ref,
                  wf_ref, o_ref, buf, sem, hs, x2_s):
    i = pl.program_id(0)
    n_steps = pl.num_programs(0)

    def gather_into(sl_ref, slot):
        def dst(q):
            t, k = q // TOP_K, q % TOP_K
            return buf.at[slot, k, t // SUBLANES, :, t % SUBLANES, :]
        return _start_row_gather(y_hbm, sl_ref, FIN_T * TOP_K, dst, sem.at[slot])

    def wait_gather(slot):
        pltpu.make_async_copy(buf.at[1 - slot], buf.at[slot], sem.at[slot]).wait()

    @pl.when(i == 0)
    def _():
        start_group = gather_into(slc_ref, 0)
        for _ in range(ISSUE_GROUPS):
            start_group()

    slot = i % 2
    start_group = gather_into(sln_ref, 1 - slot)

    h = _load_row_tiles(h2_ref, FIN_T).astype(BF16)
    for j in range(SHARED_DIM // MXU_N):
        cols = slice(j * MXU_N, (j + 1) * MXU_N)
        sg = jnp.dot(h, wsg_ref[:, cols], preferred_element_type=F32)
        start_group()
        su = jnp.dot(h, wsu_ref[:, cols], preferred_element_type=F32)
        start_group()
        hs[:, cols] = (_silu(sg) * su).astype(BF16)

    wait_gather(slot)
    gt = gt_ref[...]
    m = mod_ref[0]
    ssq = jnp.zeros((FIN_T, 1), F32)
    for n in range(D // MXU_N):
        cols = slice(n * MXU_N, (n + 1) * MXU_N)
        acc = jnp.dot(hs[...], wsd_ref[:, cols], preferred_element_type=F32)
        for k in range(TOP_K):
            yk = jnp.concatenate(
                [buf[slot, k, :, n * (MXU_N // LANES) + cc].reshape(FIN_T, LANES) for cc in range(MXU_N // LANES)],
                axis=1)
            acc = acc + gt[:, k:k + 1] * yk
        x2 = x1_ref[:, cols] + m[5:6, cols] * acc
        x2_s[:, cols] = x2
        ssq = ssq + jnp.sum(x2 * x2, axis=-1, keepdims=True)
        start_group()
        if n % 2 == 1:
            start_group()
    o_ref[...] = x2_s[...] * lax.rsqrt(ssq * (1.0 / D) + NORM_EPS) * wf_ref[...]

    @pl.when(i == n_steps - 1)
    def _():
        wait_gather(1 - slot)


def _final(slot_tk, y_p, gates_t, x1, h2, mod, wsg, wsu, wsd, wf):
    n_steps = N // FIN_T
    per_b = S // FIN_T
    sl3 = slot_tk.reshape(n_steps, 1, FIN_T * TOP_K)
    smem = pltpu.SMEM
    row = lambda i: (i, 0)
    const = lambda i: (0, 0)
    return pl.pallas_call(
        _final_kernel,
        out_shape=jax.ShapeDtypeStruct((N, D), F32),
        grid=(n_steps,),
        in_specs=[pl.BlockSpec((1, 1, FIN_T * TOP_K), lambda i: (i, 0, 0), memory_space=smem),
                  pl.BlockSpec((1, 1, FIN_T * TOP_K),
                               lambda i: (jnp.minimum(i + 1, n_steps - 1), 0, 0), memory_space=smem),
                  pl.BlockSpec(memory_space=pl.ANY),
                  pl.BlockSpec((FIN_T, TOP_K), row),
                  pl.BlockSpec((FIN_T, D), row),
                  pl.BlockSpec((FIN_T * ROW_TILES, LANES), row),
                  pl.BlockSpec((1, N_MOD, D), lambda i: (i // per_b, 0, 0)),
                  pl.BlockSpec((D, SHARED_DIM), const),
                  pl.BlockSpec((D, SHARED_DIM), const),
                  pl.BlockSpec((SHARED_DIM, D), const),
                  pl.BlockSpec((1, D), const)],
        out_specs=pl.BlockSpec((FIN_T, D), row),
        scratch_shapes=[pltpu.VMEM((2, TOP_K, FIN_T // SUBLANES, ROW_TILES, SUBLANES, LANES), F32),
                        pltpu.SemaphoreType.DMA((2,)),
                        pltpu.VMEM((FIN_T, SHARED_DIM), BF16),
                        pltpu.VMEM((FIN_T, D), F32)],
        compiler_params=_cparams(("arbitrary",), 48),
        name="final",
    )(sl3, sl3, y_p, gates_t, x1, h2, mod, wsg, wsu, wsd, wf.reshape(1, D))


def kernel(x, c, w_ada, b_ada, norm1_w, w_in, pool_w, pool_scale, w_branch_pool, lambda_q1, lambda_k1,
           lambda_q2, lambda_k2, subln_w, w_branch_attn, w_out, norm2_w, w_router, router_bias,
           w_gate_e, w_up_e, w_down_e, w_sh_gate, w_sh_up, w_sh_down, final_norm_w):
    assert x.shape == (B, S, D) and w_ada.shape[0] == 1
    x2 = x.reshape(N, D)
    mod = _ada(c, w_ada[0], b_ada[0]).reshape(B, N_MOD, D)

    h1 = _norm1(x2, norm1_w[0], mod)
    proj = _inproj(h1, w_in[0].astype(BF16), _rope_tables())
    y_pool = _pool(proj, pool_w[0], pool_scale[0])
    lam_params = jnp.stack([lambda_q1[0], lambda_k1[0], lambda_q2[0], lambda_k2[0]])
    o_attn = _attn(proj, lam_params, subln_w[0])
    x1, h2t = _merge(y_pool, o_attn, proj, x2, mod, norm2_w[0],
                     w_branch_pool[0].astype(BF16), w_branch_attn[0].astype(BF16),
                     w_out[0].astype(BF16))

    idx, gates, pos, cnt = _router(h2t, w_router[0].T, router_bias[0])
    counts = cnt[:, 0].astype(I32)
    padded = (counts + MOE_BLOCK - 1) // MOE_BLOCK * MOE_BLOCK
    pad_end = jnp.cumsum(padded)
    pad_start = pad_end - padded
    n_used = (pad_end[-1:] // MOE_BLOCK).astype(I32)
    block_row0 = jnp.arange(N_BLOCKS, dtype=I32) * MOE_BLOCK
    block_expert = jnp.minimum(jnp.sum((pad_end[None, :] <= block_row0[:, None]).astype(I32), axis=1),
                               N_EXPERTS - 1).astype(I32)
    slot = _slots(pad_start.astype(I32), idx, pos)
    tok_rows = jnp.broadcast_to(jnp.arange(N, dtype=I32)[None, :] * ROW_TILES, (TOP_K, N))
    slot_tok = jnp.zeros((N_SLOTS,), I32).at[slot.reshape(-1)].set(tok_rows.reshape(-1), unique_indices=True)

    bidx = jnp.arange(N_BLOCKS, dtype=I32)
    prev_expert = jnp.concatenate([jnp.full((1,), -1, I32), block_expert[:-1]])
    first = jnp.logical_and(bidx < n_used[0], block_expert != prev_expert)
    wslot = ((jnp.cumsum(first.astype(I32)) - 1) % 2).astype(I32)
    later_first = lax.cummin(jnp.where(first, bidx, N_BLOCKS)[::-1])[::-1]
    next_first = jnp.concatenate([later_first[1:], jnp.full((1,), N_BLOCKS, I32)])
    nxt = jnp.where(next_first < N_BLOCKS, block_expert[jnp.minimum(next_first, N_BLOCKS - 1)], -1).astype(I32)

    y_t = _moe(block_expert, n_used, first.astype(I32), wslot, nxt, slot_tok, h2t,
               w_gate_e[0], w_up_e[0], w_down_e[0])
    out = _final((slot.T * ROW_TILES).reshape(-1), y_t, gates.T, x1, h2t, mod,
                 w_sh_gate[0].astype(BF16), w_sh_up[0].astype(BF16), w_sh_down[0].astype(BF16),
                 final_norm_w)
    return out.reshape(B, S, D)
```

```python
import functools

import numpy as np
import jax
import jax.numpy as jnp
from jax import lax
from jax.experimental import pallas as pl
from jax.experimental.pallas import tpu as pltpu

D = 2048
B = 8
S = 2048
N = B * S
POOL_GROUPS = 4
POOL_DIM = 256
POOL_WIDTH = POOL_GROUPS * POOL_DIM
N_HEADS = 8
HEAD_DIM = 128
V_HEAD = 2 * HEAD_DIM
QK_WIDTH = N_HEADS * 2 * HEAD_DIM
ATTN_WIDTH = N_HEADS * V_HEAD
ROT_HALF = HEAD_DIM // 8
ROPE_THETA = 500000.0
ATTN_SCALE = HEAD_DIM ** -0.5
SUBLN_EPS = 1e-5
NORM_EPS = 1e-6
LAMBDA_INIT = 0.8 - 0.6 * float(np.exp(-0.3 * 0))
IN_WIDTH = POOL_WIDTH + 2 * QK_WIDTH + ATTN_WIDTH + 2 * D
Q_OFF = POOL_WIDTH
K_OFF = Q_OFF + QK_WIDTH
V_OFF = K_OFF + QK_WIDTH
GP_OFF = V_OFF + ATTN_WIDTH
GA_OFF = GP_OFF + D
N_EXPERTS = 64
EXPERT_DIM = 512
SHARED_DIM = 512
TOP_K = 8
N_GROUPS = 8
TOPK_GROUPS = 4
GROUP_SIZE = N_EXPERTS // N_GROUPS
ROUTED_SCALE = 2.5
MOE_BLOCK = 512
N_MOD = 6
N_SLOTS = (N * TOP_K + N_EXPERTS * (MOE_BLOCK - 1) + MOE_BLOCK - 1) // MOE_BLOCK * MOE_BLOCK
N_BLOCKS = N_SLOTS // MOE_BLOCK
HALF = D // 2

LANES = 128
V7X_VMEM_LIMIT = 60000 * 1024

BF16 = jnp.bfloat16
F32 = jnp.float32
I32 = jnp.int32
U32 = jnp.uint32


def _cparams(sem, vmem_mb):
    return pltpu.CompilerParams(dimension_semantics=sem,
                                vmem_limit_bytes=min(vmem_mb * 1024 * 1024, V7X_VMEM_LIMIT))


ROW_TILES = D // LANES


def _store_row_tiles(ref, vals):
    rows = vals.shape[0]
    for c in range(ROW_TILES):
        ref[pl.ds(c, rows, stride=ROW_TILES), :] = vals[:, c * LANES:(c + 1) * LANES]


def _load_row_tiles(ref, rows):
    return jnp.concatenate([ref[pl.ds(c, rows, stride=ROW_TILES), :] for c in range(ROW_TILES)], axis=1)


def _silu(x):
    return x * jax.nn.sigmoid(x)


ADA_TN = 512


def _ada_kernel(c_ref, w_ref, b_ref, o_ref):
    c = c_ref[...]
    ca = _silu(c).astype(BF16)
    o_ref[...] = jnp.dot(ca, w_ref[...].astype(BF16), preferred_element_type=F32) + b_ref[...]


def _ada(c, w_ada, b_ada):
    width = N_MOD * D
    return pl.pallas_call(
        _ada_kernel,
        out_shape=jax.ShapeDtypeStruct((B, width), F32),
        grid=(width // ADA_TN,),
        in_specs=[pl.BlockSpec((B, D), lambda j: (0, 0)),
                  pl.BlockSpec((D, ADA_TN), lambda j: (0, j)),
                  pl.BlockSpec((1, ADA_TN), lambda j: (0, j))],
        out_specs=pl.BlockSpec((B, ADA_TN), lambda j: (0, j)),
        compiler_params=_cparams(("arbitrary",), 24),
        name="ada",
    )(c, w_ada, b_ada.reshape(1, width))


NORM_TR = 512


def _norm1_kernel(x_ref, w_ref, mod_ref, h_ref):
    x = x_ref[...]
    ms = jnp.mean(x * x, axis=-1, keepdims=True)
    y = x * lax.rsqrt(ms + NORM_EPS) * w_ref[...]
    m = mod_ref[0]
    h_ref[...] = (y * (1.0 + m[1:2]) + m[0:1]).astype(BF16)


def _norm1(x2, norm_w, mod):
    per_b = S // NORM_TR
    return pl.pallas_call(
        _norm1_kernel,
        out_shape=jax.ShapeDtypeStruct((N, D), BF16),
        grid=(N // NORM_TR,),
        in_specs=[pl.BlockSpec((NORM_TR, D), lambda i: (i, 0)),
                  pl.BlockSpec((1, D), lambda i: (0, 0)),
                  pl.BlockSpec((1, N_MOD, D), lambda i: (i // per_b, 0, 0))],
        out_specs=pl.BlockSpec((NORM_TR, D), lambda i: (i, 0)),
        compiler_params=_cparams(("arbitrary",), 32),
        name="norm1",
    )(x2, norm_w.reshape(1, D), mod)


INP_TM = 1024
INP_TN = 512
ROPE_J0 = Q_OFF // INP_TN
ROPE_JK = K_OFF // INP_TN
ROPE_J1 = V_OFF // INP_TN


def _rope_tables():
    pos = np.arange(S, dtype=np.float64)
    inv_freq = 1.0 / (ROPE_THETA ** (np.arange(0, 2 * ROT_HALF, 2, dtype=np.float64) / (2 * ROT_HALF)))
    ang = pos[:, None] * inv_freq[None, :]
    cos, sin = np.cos(ang), np.sin(ang)
    t0 = np.ones((S, HEAD_DIM)); t1 = np.zeros((S, HEAD_DIM)); t2 = np.zeros((S, HEAD_DIM))
    t0[:, :ROT_HALF] = cos; t0[:, ROT_HALF:2 * ROT_HALF] = cos
    t1[:, ROT_HALF:2 * ROT_HALF] = sin
    t2[:, :ROT_HALF] = -sin
    k_tab = np.stack([t0, t1, t2])
    return jnp.asarray(np.stack([k_tab * (ATTN_SCALE * np.log2(np.e)), k_tab]), dtype=F32)


def _inproj_kernel(h_ref, w_ref, tab_ref, o_ref):
    j = pl.program_id(1)
    acc = jnp.dot(h_ref[...], w_ref[...], preferred_element_type=F32)
    is_rope = jnp.logical_and(j >= ROPE_J0, j < ROPE_J1)

    @pl.when(is_rope)
    def _():
        t0 = tab_ref[0, 0]
        t1 = tab_ref[0, 1]
        t2 = tab_ref[0, 2]
        for s in range(INP_TN // HEAD_DIM):
            a = acc[:, s * HEAD_DIM:(s + 1) * HEAD_DIM]
            r = a * t0 + pltpu.roll(a, ROT_HALF, 1) * t1 + pltpu.roll(a, HEAD_DIM - ROT_HALF, 1) * t2
            o_ref[:, s * HEAD_DIM:(s + 1) * HEAD_DIM] = r.astype(BF16)

    @pl.when(jnp.logical_not(is_rope))
    def _():
        o_ref[...] = acc.astype(BF16)


def _inproj(h1, w_in_bf, tabs):
    per_b = S // INP_TM
    return pl.pallas_call(
        _inproj_kernel,
        out_shape=jax.ShapeDtypeStruct((N, IN_WIDTH), BF16),
        grid=(N // INP_TM, IN_WIDTH // INP_TN),
        in_specs=[pl.BlockSpec((INP_TM, D), lambda i, j: (i, 0)),
                  pl.BlockSpec((D, INP_TN), lambda i, j: (0, j)),
                  pl.BlockSpec((1, 3, INP_TM, HEAD_DIM),
                               lambda i, j: (jnp.where(j >= ROPE_JK, 1, 0), 0, i % per_b, 0))],
        out_specs=pl.BlockSpec((INP_TM, INP_TN), lambda i, j: (i, j)),
        compiler_params=_cparams(("arbitrary", "arbitrary"), 48),
        name="inproj",
    )(h1, w_in_bf, tabs)


POOL_PAD = 16


def _pool_kernel(u_ref, w_ref, sc_ref, o_ref, pad_ref):
    g = pl.program_id(1)
    u = u_ref[...].astype(F32)
    pad_ref[0:POOL_PAD, :] = jnp.zeros((POOL_PAD, POOL_DIM), F32)
    sums = []
    s = u
    for sh in (1, 2, 4, 8):
        pad_ref[POOL_PAD:POOL_PAD + S, :] = s
        s = s + pad_ref[POOL_PAD - sh:POOL_PAD - sh + S, :]
        sums.append(s)
    win_sum = jnp.where(g == 0, sums[0], jnp.where(g == 1, sums[1], jnp.where(g == 2, sums[2], sums[3])))
    win = jnp.left_shift(2, g)
    row = lax.broadcasted_iota(I32, (S, POOL_DIM), 0)
    cnt = jnp.minimum(row + 1, win).astype(F32)
    mixed = (win_sum / cnt - u).astype(BF16)
    y = jnp.dot(mixed, w_ref[0].astype(BF16), preferred_element_type=F32) * sc_ref[0]
    o_ref[...] = y.astype(BF16)


def _pool(proj, pool_w, pool_scale):
    return pl.pallas_call(
        _pool_kernel,
        out_shape=jax.ShapeDtypeStruct((N, POOL_WIDTH), BF16),
        grid=(B, POOL_GROUPS),
        in_specs=[pl.BlockSpec((S, POOL_DIM), lambda b, g: (b, g)),
                  pl.BlockSpec((1, POOL_DIM, POOL_DIM), lambda b, g: (g, 0, 0)),
                  pl.BlockSpec((1, 1, POOL_DIM), lambda b, g: (g, 0, 0))],
        out_specs=pl.BlockSpec((S, POOL_DIM), lambda b, g: (b, g)),
        scratch_shapes=[pltpu.VMEM((POOL_PAD + S, POOL_DIM), F32)],
        compiler_params=_cparams(("arbitrary", "arbitrary"), 40),
        name="pool",
    )(proj, pool_w, pool_scale.reshape(POOL_GROUPS, 1, POOL_DIM))


ATT_T = 512


def _attn_kernel(q_ref, k_ref, v_ref, lam_ref, sw_ref, o_ref, s_ref, mx_ref, ls_ref, acc_ref):
    qi = pl.program_id(2)
    q = q_ref[...]
    qs = (q[:, :HEAD_DIM], q[:, HEAD_DIM:])
    mx_ref[...] = jnp.full(mx_ref.shape, -jnp.inf, F32)
    ls_ref[...] = jnp.zeros(ls_ref.shape, F32)
    acc_ref[...] = jnp.zeros(acc_ref.shape, F32)

    def scores(j, masked):
        start = pl.multiple_of(j * ATT_T, ATT_T)
        ks = k_ref[pl.ds(start, ATT_T), :]
        for c in range(2):
            s = lax.dot_general(qs[c], ks[:, c * HEAD_DIM:(c + 1) * HEAD_DIM],
                                (((1,), (1,)), ((), ())), preferred_element_type=F32)
            if masked:
                row = lax.broadcasted_iota(I32, (ATT_T, ATT_T), 0)
                col = lax.broadcasted_iota(I32, (ATT_T, ATT_T), 1)
                s = jnp.where(col <= row, s, -jnp.inf)
            s_ref[c, j] = s
            mx = mx_ref[c]
            for t in range(ATT_T // LANES):
                mx = jnp.maximum(mx, s[:, t * LANES:(t + 1) * LANES])
            mx_ref[c] = mx

    def scores_body(j, carry):
        scores(j, False)
        return carry

    lax.fori_loop(0, qi, scores_body, 0)
    scores(qi, True)

    row_max = [jnp.max(mx_ref[c], axis=-1, keepdims=True) for c in range(2)]

    def pv_body(j, carry):
        start = pl.multiple_of(j * ATT_T, ATT_T)
        vs = v_ref[pl.ds(start, ATT_T), :]
        for c in range(2):
            p = jnp.exp2(s_ref[c, j] - row_max[c])
            ls = ls_ref[c]
            for t in range(ATT_T // LANES):
                ls = ls + p[:, t * LANES:(t + 1) * LANES]
            ls_ref[c] = ls
            acc_ref[c] += jnp.dot(p.astype(BF16), vs, preferred_element_type=F32)
        return carry

    lax.fori_loop(0, qi + 1, pv_body, 0)

    lp = lam_ref[...]
    lam = (jnp.exp(jnp.sum(lp[0:1] * lp[1:2], axis=-1, keepdims=True))
           - jnp.exp(jnp.sum(lp[2:3] * lp[3:4], axis=-1, keepdims=True)) + LAMBDA_INIT)
    l1 = jnp.sum(ls_ref[0], axis=-1, keepdims=True)
    l2 = jnp.sum(ls_ref[1], axis=-1, keepdims=True)
    o = acc_ref[0] / l1 - lam * (acc_ref[1] / l2)
    ms = jnp.mean(o * o, axis=-1, keepdims=True)
    o = o * lax.rsqrt(ms + SUBLN_EPS) * sw_ref[...] * (1.0 - LAMBDA_INIT)
    o_ref[...] = o.astype(BF16)


def _attn(proj, lam_params, subln_w):
    nq = S // ATT_T
    qb, kb, vb = Q_OFF // V_HEAD, K_OFF // V_HEAD, V_OFF // V_HEAD
    return pl.pallas_call(
        _attn_kernel,
        out_shape=jax.ShapeDtypeStruct((N, ATTN_WIDTH), BF16),
        grid=(B, N_HEADS, nq),
        in_specs=[pl.BlockSpec((ATT_T, V_HEAD), lambda b, h, i: (b * nq + i, qb + h)),
                  pl.BlockSpec((S, V_HEAD), lambda b, h, i: (b, kb + h)),
                  pl.BlockSpec((S, V_HEAD), lambda b, h, i: (b, vb + h)),
                  pl.BlockSpec((4, HEAD_DIM), lambda b, h, i: (0, 0)),
                  pl.BlockSpec((1, V_HEAD), lambda b, h, i: (0, 0))],
        out_specs=pl.BlockSpec((ATT_T, V_HEAD), lambda b, h, i: (b * nq + i, h)),
        scratch_shapes=[pltpu.VMEM((2, S // ATT_T, ATT_T, ATT_T), F32),
                        pltpu.VMEM((2, ATT_T, LANES), F32),
                        pltpu.VMEM((2, ATT_T, LANES), F32),
                        pltpu.VMEM((2, ATT_T, V_HEAD), F32)],
        compiler_params=_cparams(("arbitrary", "arbitrary", "arbitrary"), 40),
        name="attn",
    )(proj, proj, proj, lam_params, subln_w.reshape(1, V_HEAD))


MRG_TM = 256


def _merge_kernel(yp_ref, o_ref, gp0_ref, gp1_ref, ga0_ref, ga1_ref, x_ref, mod_ref, w2_ref,
                  wbp_ref, wba_ref, wout_ref, x1_ref, h2t_ref):
    a1 = jnp.dot(yp_ref[...], wbp_ref[...], preferred_element_type=F32)
    a2 = jnp.dot(o_ref[...], wba_ref[...], preferred_element_type=F32)
    m_lo = (jax.nn.sigmoid(gp0_ref[...].astype(F32)) * a1[:, :HALF]
            + jax.nn.sigmoid(ga0_ref[...].astype(F32)) * a2[:, :HALF]).astype(BF16)
    m_hi = (jax.nn.sigmoid(gp1_ref[...].astype(F32)) * a1[:, HALF:]
            + jax.nn.sigmoid(ga1_ref[...].astype(F32)) * a2[:, HALF:]).astype(BF16)
    out = (jnp.dot(m_lo, wout_ref[0:HALF, :], preferred_element_type=F32)
           + jnp.dot(m_hi, wout_ref[HALF:D, :], preferred_element_type=F32))
    m = mod_ref[0]
    x1 = x_ref[...] + m[2:3] * out
    x1_ref[...] = x1
    ms = jnp.mean(x1 * x1, axis=-1, keepdims=True)
    h2 = x1 * lax.rsqrt(ms + NORM_EPS) * w2_ref[...] * (1.0 + m[4:5]) + m[3:4]
    _store_row_tiles(h2t_ref, h2)


def _merge(y_pool, o_attn, proj, x2, mod, norm2_w, wbp, wba, wout):
    per_b = S // MRG_TM
    gpb, gab = GP_OFF // HALF, GA_OFF // HALF
    row = lambda i: (i, 0)
    const = lambda i: (0, 0)
    wspec = lambda shape: pl.BlockSpec(shape, const, pipeline_mode=pl.Buffered(1))
    return pl.pallas_call(
        _merge_kernel,
        out_shape=(jax.ShapeDtypeStruct((N, D), F32),
                   jax.ShapeDtypeStruct((N * ROW_TILES, LANES), F32)),
        grid=(N // MRG_TM,),
        in_specs=[pl.BlockSpec((MRG_TM, POOL_WIDTH), row),
                  pl.BlockSpec((MRG_TM, ATTN_WIDTH), row),
                  pl.BlockSpec((MRG_TM, HALF), lambda i: (i, gpb)),
                  pl.BlockSpec((MRG_TM, HALF), lambda i: (i, gpb + 1)),
                  pl.BlockSpec((MRG_TM, HALF), lambda i: (i, gab)),
                  pl.BlockSpec((MRG_TM, HALF), lambda i: (i, gab + 1)),
                  pl.BlockSpec((MRG_TM, D), row),
                  pl.BlockSpec((1, N_MOD, D), lambda i: (i // per_b, 0, 0)),
                  pl.BlockSpec((1, D), const),
                  wspec((POOL_WIDTH, D)),
                  wspec((ATTN_WIDTH, D)),
                  wspec((D, D))],
        out_specs=(pl.BlockSpec((MRG_TM, D), row),
                   pl.BlockSpec((MRG_TM * ROW_TILES, LANES), row)),
        compiler_params=_cparams(("arbitrary",), 58),
        name="merge",
    )(y_pool, o_attn, proj, proj, proj, proj, x2, mod, norm2_w.reshape(1, D), wbp, wba, wout)


RT_T = 512


def _router_kernel(h_ref, wr_ref, bias_ref, idx_ref, gate_ref, pos_ref, cnt_ref, carry_ref):
    i = pl.program_id(0)

    @pl.when(i == 0)
    def _():
        carry_ref[...] = jnp.zeros(carry_ref.shape, F32)

    logits = lax.dot_general(wr_ref[...], _load_row_tiles(h_ref, RT_T), (((1,), (1,)), ((), ())),
                             precision=lax.Precision.HIGHEST, preferred_element_type=F32)
    scores = jax.nn.sigmoid(logits)
    biased = scores + bias_ref[...]
    neg = -jnp.inf

    b3 = biased.reshape(N_GROUPS, GROUP_SIZE, RT_T)
    io3 = lax.broadcasted_iota(I32, b3.shape, 1)
    m1 = jnp.max(b3, axis=1, keepdims=True)
    i1 = jnp.min(jnp.where(b3 == m1, io3, GROUP_SIZE), axis=1, keepdims=True)
    m2 = jnp.max(jnp.where(io3 == i1, neg, b3), axis=1, keepdims=True)
    gs = m1 + m2

    gio = lax.broadcasted_iota(I32, gs.shape, 0)
    gsel = jnp.zeros(gs.shape, jnp.bool_)
    cur = gs
    for _ in range(TOPK_GROUPS):
        mx = jnp.max(cur, axis=0, keepdims=True)
        ix = jnp.min(jnp.where(cur == mx, gio, N_GROUPS), axis=0, keepdims=True)
        pick = gio == ix
        gsel = jnp.logical_or(gsel, pick)
        cur = jnp.where(pick, neg, cur)

    masked = jnp.where(gsel, b3, neg).reshape(N_EXPERTS, RT_T)
    eio = lax.broadcasted_iota(I32, (N_EXPERTS, RT_T), 0)
    assign = jnp.zeros((N_EXPERTS, RT_T), F32)
    idxs, sels = [], []
    for _ in range(TOP_K):
        mx = jnp.max(masked, axis=0, keepdims=True)
        ix = jnp.min(jnp.where(masked == mx, eio, N_EXPERTS), axis=0, keepdims=True)
        hot = eio == ix
        sels.append(jnp.sum(jnp.where(hot, scores, 0.0), axis=0, keepdims=True))
        idxs.append(ix)
        masked = jnp.where(hot, neg, masked)
        assign = jnp.where(hot, 1.0, assign)

    denom = sels[0]
    for k in range(1, TOP_K):
        denom = denom + sels[k]

    tp = lax.broadcasted_iota(I32, (RT_T, RT_T), 0)
    tc = lax.broadcasted_iota(I32, (RT_T, RT_T), 1)
    before = jnp.where(tp < tc, 1.0, 0.0).astype(BF16)
    rank = jnp.dot(assign.astype(BF16), before, preferred_element_type=F32) + carry_ref[...]
    for k in range(TOP_K):
        idx_ref[k:k + 1, :] = idxs[k]
        gate_ref[k:k + 1, :] = sels[k] / denom * ROUTED_SCALE
        pos_ref[k:k + 1, :] = jnp.sum(jnp.where(eio == idxs[k], rank, 0.0), axis=0,
                                      keepdims=True).astype(I32)
    carry = carry_ref[...] + jnp.sum(assign, axis=1, keepdims=True)
    carry_ref[...] = carry
    cnt_ref[...] = jnp.broadcast_to(carry, cnt_ref.shape)


def _router(h2, w_router_t, bias):
    tok = lambda i: (0, i)
    return pl.pallas_call(
        _router_kernel,
        out_shape=(jax.ShapeDtypeStruct((TOP_K, N), I32),
                   jax.ShapeDtypeStruct((TOP_K, N), F32),
                   jax.ShapeDtypeStruct((TOP_K, N), I32),
                   jax.ShapeDtypeStruct((N_EXPERTS, LANES), F32)),
        grid=(N // RT_T,),
        in_specs=[pl.BlockSpec((RT_T * ROW_TILES, LANES), lambda i: (i, 0)),
                  pl.BlockSpec((N_EXPERTS, D), lambda i: (0, 0)),
                  pl.BlockSpec((N_EXPERTS, 1), lambda i: (0, 0))],
        out_specs=(pl.BlockSpec((TOP_K, RT_T), tok),
                   pl.BlockSpec((TOP_K, RT_T), tok),
                   pl.BlockSpec((TOP_K, RT_T), tok),
                   pl.BlockSpec((N_EXPERTS, LANES), lambda i: (0, 0))),
        scratch_shapes=[pltpu.VMEM((N_EXPERTS, 1), F32)],
        compiler_params=_cparams(("arbitrary",), 32),
        name="router",
    )(h2, w_router_t, bias.reshape(N_EXPERTS, 1))


SLOT_T = 2048


def _slot_kernel(ps_ref, idx_ref, pos_ref, slot_ref):
    idx = idx_ref[...]
    base = jnp.zeros(idx.shape, I32)
    for e in range(N_EXPERTS):
        base = jnp.where(idx == e, ps_ref[e], base)
    slot_ref[...] = base + pos_ref[...]


def _slots(pad_start, idx, pos):
    tok = lambda i, ps: (0, i)
    return pl.pallas_call(
        _slot_kernel,
        out_shape=jax.ShapeDtypeStruct((TOP_K, N), I32),
        grid_spec=pltpu.PrefetchScalarGridSpec(
            num_scalar_prefetch=1, grid=(N // SLOT_T,),
            in_specs=[pl.BlockSpec((TOP_K, SLOT_T), tok), pl.BlockSpec((TOP_K, SLOT_T), tok)],
            out_specs=pl.BlockSpec((TOP_K, SLOT_T), tok)),
        compiler_params=_cparams(("arbitrary",), 16),
        name="slots",
    )(pad_start, idx, pos)


SUBLANES = 8
MXU_N = 256
ISSUE_GROUPS = 64


def _start_row_gather(src_hbm, off_ref, n_rows, dst_of_row, sem):
    per_group = n_rows // ISSUE_GROUPS
    state = {"next": 0}

    def start_group():
        g = state["next"]
        state["next"] = g + 1
        for r in range(g * per_group, (g + 1) * per_group):
            src = pl.multiple_of(off_ref[0, 0, r], ROW_TILES)
            pltpu.make_async_copy(src_hbm.at[pl.ds(src, ROW_TILES), :], dst_of_row(r), sem).start(
                priority=r % 2)
    return start_group


def _moe_kernel(be_ref, nu_ref, first_ref, wslot_ref, nxt_ref, tokc_ref, tokn_ref, h_hbm,
                wg_hbm, wu_hbm, wd_hbm, y_ref, buf, sem, wst_g, wst_u, wst_d, wsem, wg_s, wu_s, wd_s, xs, hs):
    b = pl.program_id(0)
    n_used = nu_ref[0]

    def weight_copies(e, ws):
        return (pltpu.make_async_copy(wg_hbm.at[e], wst_g.at[ws], wsem.at[ws, 0]),
                pltpu.make_async_copy(wu_hbm.at[e], wst_u.at[ws], wsem.at[ws, 1]),
                pltpu.make_async_copy(wd_hbm.at[e], wst_d.at[ws], wsem.at[ws, 2]))

    def gather_into(tok_ref, slot):
        return _start_row_gather(h_hbm, tok_ref, MOE_BLOCK,
                                 lambda r: buf.at[slot, r // SUBLANES, :, r % SUBLANES, :], sem.at[slot])

    def wait_gather(slot):
        pltpu.make_async_copy(buf.at[1 - slot], buf.at[slot], sem.at[slot]).wait()

    @pl.when(jnp.logical_and(b == 0, n_used > 0))
    def _():
        for cp in weight_copies(be_ref[0], 0):
            cp.start()

    @pl.when(b == 0)
    def _():
        start_group = gather_into(tokc_ref, 0)
        for _ in range(ISSUE_GROUPS):
            start_group()

    @pl.when(first_ref[b] == 1)
    def _():
        ws = wslot_ref[b]
        for cp in weight_copies(be_ref[b], ws):
            cp.wait()
        wg_s[...] = wst_g[ws].astype(BF16)
        wu_s[...] = wst_u[ws].astype(BF16)
        wd_s[...] = wst_d[ws].astype(BF16)
        nxt = nxt_ref[b]

        @pl.when(nxt >= 0)
        def _():
            for cp in weight_copies(nxt, 1 - ws):
                cp.start()

    @pl.when(b < n_used)
    def _():
        slot = b % 2
        wait_gather(slot)
        start_group = gather_into(tokn_ref, 1 - slot)
        def dot_pieces(lhs_ref, w_ref, cols, acc=None):
            for kc in range(lhs_ref.shape[1] // MXU_N):
                rows = slice(kc * MXU_N, (kc + 1) * MXU_N)
                piece = jnp.dot(lhs_ref[:, rows], w_ref[rows, cols], preferred_element_type=F32)
                acc = piece if acc is None else acc + piece
                start_group()
            return acc

        for c in range(ROW_TILES):
            xs[:, c * LANES:(c + 1) * LANES] = buf[slot, :, c].reshape(MOE_BLOCK, LANES).astype(BF16)
            start_group()
        for j in range(EXPERT_DIM // MXU_N):
            cols = slice(j * MXU_N, (j + 1) * MXU_N)
            g = dot_pieces(xs, wg_s, cols)
            u = dot_pieces(xs, wu_s, cols)
            hs[:, cols] = (_silu(g) * u).astype(BF16)
        for n in range(D // MXU_N):
            yv = dot_pieces(hs, wd_s, slice(n * MXU_N, (n + 1) * MXU_N))
            for cc in range(MXU_N // LANES):
                c = n * (MXU_N // LANES) + cc
                y_ref[pl.ds(c, MOE_BLOCK, stride=ROW_TILES), :] = yv[:, cc * LANES:(cc + 1) * LANES]

    @pl.when(b == n_used)
    def _():
        wait_gather(b % 2)

    @pl.when(b >= n_used)
    def _():
        y_ref[...] = jnp.zeros(y_ref.shape, F32)


def _moe(block_expert, n_used, first, wslot, nxt, slot_tok, h2t, wg, wu, wd):
    tok3 = slot_tok.reshape(N_BLOCKS, 1, MOE_BLOCK)
    smem = pltpu.SMEM
    cur = lambda b, *_: (b, 0, 0)
    nxt_blk = lambda b, *_: (jnp.minimum(b + 1, N_BLOCKS - 1), 0, 0)
    any_spec = pl.BlockSpec(memory_space=pl.ANY)
    return pl.pallas_call(
        _moe_kernel,
        out_shape=jax.ShapeDtypeStruct((N_SLOTS * ROW_TILES, LANES), F32),
        grid_spec=pltpu.PrefetchScalarGridSpec(
            num_scalar_prefetch=5, grid=(N_BLOCKS,),
            in_specs=[pl.BlockSpec((1, 1, MOE_BLOCK), cur, memory_space=smem),
                      pl.BlockSpec((1, 1, MOE_BLOCK), nxt_blk, memory_space=smem),
                      any_spec, any_spec, any_spec, any_spec],
            out_specs=pl.BlockSpec((MOE_BLOCK * ROW_TILES, LANES), lambda b, *_: (b, 0)),
            scratch_shapes=[pltpu.VMEM((2, MOE_BLOCK // SUBLANES, ROW_TILES, SUBLANES, LANES), F32),
                            pltpu.SemaphoreType.DMA((2,)),
                            pltpu.VMEM((2, D, EXPERT_DIM), F32),
                            pltpu.VMEM((2, D, EXPERT_DIM), F32),
                            pltpu.VMEM((2, EXPERT_DIM, D), F32),
                            pltpu.SemaphoreType.DMA((2, 3)),
                            pltpu.VMEM((D, EXPERT_DIM), BF16),
                            pltpu.VMEM((D, EXPERT_DIM), BF16),
                            pltpu.VMEM((EXPERT_DIM, D), BF16),
                            pltpu.VMEM((MOE_BLOCK, D), BF16),
                            pltpu.VMEM((MOE_BLOCK, EXPERT_DIM), BF16)]),
        compiler_params=_cparams(("arbitrary",), 58),
        name="moe",
    )(block_expert, n_used, first, wslot, nxt, tok3, tok3, h2t, wg, wu, wd)


FIN_T = 128


def _final_kernel(slc_ref, sln_ref, y_hbm, gt_ref, x1_ref, h2_ref, mod_ref, wsg_ref, wsu_ref, wsd_ref,
                  wf_ref, o_ref, buf, sem, hs, x2_s, hb):
    i = pl.program_id(0)
    n_steps = pl.num_programs(0)

    def gather_into(sl_ref, slot):
        def dst(q):
            t, k = q // TOP_K, q % TOP_K
            return buf.at[slot, k, t // SUBLANES, :, t % SUBLANES, :]
        return _start_row_gather(y_hbm, sl_ref, FIN_T * TOP_K, dst, sem.at[slot])

    def wait_gather(slot):
        pltpu.make_async_copy(buf.at[1 - slot], buf.at[slot], sem.at[slot]).wait()

    @pl.when(i == 0)
    def _():
        start_group = gather_into(slc_ref, 0)
        for _ in range(ISSUE_GROUPS):
            start_group()

    slot = i % 2
    start_group = gather_into(sln_ref, 1 - slot)

    hb[...] = _load_row_tiles(h2_ref, FIN_T).astype(BF16)

    def dot_pieces(w_ref, cols):
        acc = None
        for kc in range(D // MXU_N):
            rows = slice(kc * MXU_N, (kc + 1) * MXU_N)
            piece = jnp.dot(hb[:, rows], w_ref[rows, cols], preferred_element_type=F32)
            acc = piece if acc is None else acc + piece
            start_group()
        return acc

    for j in range(SHARED_DIM // MXU_N):
        cols = slice(j * MXU_N, (j + 1) * MXU_N)
        sg = dot_pieces(wsg_ref, cols)
        su = dot_pieces(wsu_ref, cols)
        hs[:, cols] = (_silu(sg) * su).astype(BF16)

    wait_gather(slot)
    gt = gt_ref[...]
    m = mod_ref[0]
    ssq = jnp.zeros((FIN_T, 1), F32)
    for n in range(D // MXU_N):
        cols = slice(n * MXU_N, (n + 1) * MXU_N)
        acc = jnp.dot(hs[...], wsd_ref[:, cols], preferred_element_type=F32)
        for k in range(TOP_K):
            yk = jnp.concatenate(
                [buf[slot, k, :, n * (MXU_N // LANES) + cc].reshape(FIN_T, LANES) for cc in range(MXU_N // LANES)],
                axis=1)
            acc = acc + gt[:, k:k + 1] * yk
            if k % 2 == 1:
                start_group()
        x2 = x1_ref[:, cols] + m[5:6, cols] * acc
        x2_s[:, cols] = x2
        ssq = ssq + jnp.sum(x2 * x2, axis=-1, keepdims=True)
    o_ref[...] = x2_s[...] * lax.rsqrt(ssq * (1.0 / D) + NORM_EPS) * wf_ref[...]

    @pl.when(i == n_steps - 1)
    def _():
        wait_gather(1 - slot)


def _final(slot_tk, y_p, gates_t, x1, h2, mod, wsg, wsu, wsd, wf):
    n_steps = N // FIN_T
    per_b = S // FIN_T
    sl3 = slot_tk.reshape(n_steps, 1, FIN_T * TOP_K)
    smem = pltpu.SMEM
    row = lambda i: (i, 0)
    const = lambda i: (0, 0)
    return pl.pallas_call(
        _final_kernel,
        out_shape=jax.ShapeDtypeStruct((N, D), F32),
        grid=(n_steps,),
        in_specs=[pl.BlockSpec((1, 1, FIN_T * TOP_K), lambda i: (i, 0, 0), memory_space=smem),
                  pl.BlockSpec((1, 1, FIN_T * TOP_K),
                               lambda i: (jnp.minimum(i + 1, n_steps - 1), 0, 0), memory_space=smem),
                  pl.BlockSpec(memory_space=pl.ANY),
                  pl.BlockSpec((FIN_T, TOP_K), row),
                  pl.BlockSpec((FIN_T, D), row),
                  pl.BlockSpec((FIN_T * ROW_TILES, LANES), row),
                  pl.BlockSpec((1, N_MOD, D), lambda i: (i // per_b, 0, 0)),
                  pl.BlockSpec((D, SHARED_DIM), const),
                  pl.BlockSpec((D, SHARED_DIM), const),
                  pl.BlockSpec((SHARED_DIM, D), const),
                  pl.BlockSpec((1, D), const)],
        out_specs=pl.BlockSpec((FIN_T, D), row),
        scratch_shapes=[pltpu.VMEM((2, TOP_K, FIN_T // SUBLANES, ROW_TILES, SUBLANES, LANES), F32),
                        pltpu.SemaphoreType.DMA((2,)),
                        pltpu.VMEM((FIN_T, SHARED_DIM), BF16),
                        pltpu.VMEM((FIN_T, D), F32),
                        pltpu.VMEM((FIN_T, D), BF16)],
        compiler_params=_cparams(("arbitrary",), 48),
        name="final",
    )(sl3, sl3, y_p, gates_t, x1, h2, mod, wsg, wsu, wsd, wf.reshape(1, D))


def kernel(x, c, w_ada, b_ada, norm1_w, w_in, pool_w, pool_scale, w_branch_pool, lambda_q1, lambda_k1,
           lambda_q2, lambda_k2, subln_w, w_branch_attn, w_out, norm2_w, w_router, router_bias,
           w_gate_e, w_up_e, w_down_e, w_sh_gate, w_sh_up, w_sh_down, final_norm_w):
    assert x.shape == (B, S, D) and w_ada.shape[0] == 1
    x2 = x.reshape(N, D)
    mod = _ada(c, w_ada[0], b_ada[0]).reshape(B, N_MOD, D)

    h1 = _norm1(x2, norm1_w[0], mod)
    proj = _inproj(h1, w_in[0].astype(BF16), _rope_tables())
    y_pool = _pool(proj, pool_w[0], pool_scale[0])
    lam_params = jnp.stack([lambda_q1[0], lambda_k1[0], lambda_q2[0], lambda_k2[0]])
    o_attn = _attn(proj, lam_params, subln_w[0])
    x1, h2t = _merge(y_pool, o_attn, proj, x2, mod, norm2_w[0],
                     w_branch_pool[0].astype(BF16), w_branch_attn[0].astype(BF16),
                     w_out[0].astype(BF16))

    idx, gates, pos, cnt = _router(h2t, w_router[0].T, router_bias[0])
    counts = cnt[:, 0].astype(I32)
    padded = (counts + MOE_BLOCK - 1) // MOE_BLOCK * MOE_BLOCK
    pad_end = jnp.cumsum(padded)
    pad_start = pad_end - padded
    n_used = (pad_end[-1:] // MOE_BLOCK).astype(I32)
    block_row0 = jnp.arange(N_BLOCKS, dtype=I32) * MOE_BLOCK
    block_expert = jnp.minimum(jnp.sum((pad_end[None, :] <= block_row0[:, None]).astype(I32), axis=1),
                               N_EXPERTS - 1).astype(I32)
    slot = _slots(pad_start.astype(I32), idx, pos)
    tok_rows = jnp.broadcast_to(jnp.arange(N, dtype=I32)[None, :] * ROW_TILES, (TOP_K, N))
    slot_tok = jnp.zeros((N_SLOTS,), I32).at[slot.reshape(-1)].set(tok_rows.reshape(-1), unique_indices=True)

    bidx = jnp.arange(N_BLOCKS, dtype=I32)
    prev_expert = jnp.concatenate([jnp.full((1,), -1, I32), block_expert[:-1]])
    first = jnp.logical_and(bidx < n_used[0], block_expert != prev_expert)
    wslot = ((jnp.cumsum(first.astype(I32)) - 1) % 2).astype(I32)
    later_first = lax.cummin(jnp.where(first, bidx, N_BLOCKS)[::-1])[::-1]
    next_first = jnp.concatenate([later_first[1:], jnp.full((1,), N_BLOCKS, I32)])
    nxt = jnp.where(next_first < N_BLOCKS, block_expert[jnp.minimum(next_first, N_BLOCKS - 1)], -1).astype(I32)

    y_t = _moe(block_expert, n_used, first.astype(I32), wslot, nxt, slot_tok, h2t,
               w_gate_e[0], w_up_e[0], w_down_e[0])
    out = _final((slot.T * ROW_TILES).reshape(-1), y_t, gates.T, x1, h2t, mod,
                 w_sh_gate[0].astype(BF16), w_sh_up[0].astype(BF16), w_sh_down[0].astype(BF16),
                 final_norm_w)
    return out.reshape(B, S, D)
```

```python
import numpy as np
import jax
import jax.numpy as jnp
from jax import lax
from jax.experimental import pallas as pl
from jax.experimental.pallas import tpu as pltpu

D = 2048
B = 8
S = 2048
N = B * S
POOL_GROUPS = 4
POOL_DIM = 256
POOL_WIDTH = POOL_GROUPS * POOL_DIM
N_HEADS = 8
HEAD_DIM = 128
V_HEAD = 2 * HEAD_DIM
QK_WIDTH = N_HEADS * 2 * HEAD_DIM
ATTN_WIDTH = N_HEADS * V_HEAD
ROT_HALF = HEAD_DIM // 8
ROPE_THETA = 500000.0
ATTN_SCALE = HEAD_DIM ** -0.5
SUBLN_EPS = 1e-5
NORM_EPS = 1e-6
LAMBDA_INIT = 0.8 - 0.6 * float(np.exp(-0.3 * 0))
IN_WIDTH = POOL_WIDTH + 2 * QK_WIDTH + ATTN_WIDTH + 2 * D
Q_OFF = POOL_WIDTH
K_OFF = Q_OFF + QK_WIDTH
V_OFF = K_OFF + QK_WIDTH
GP_OFF = V_OFF + ATTN_WIDTH
GA_OFF = GP_OFF + D
N_EXPERTS = 64
EXPERT_DIM = 512
SHARED_DIM = 512
TOP_K = 8
N_GROUPS = 8
TOPK_GROUPS = 4
GROUP_SIZE = N_EXPERTS // N_GROUPS
ROUTED_SCALE = 2.5
MOE_BLOCK = 512
N_MOD = 6
N_SLOTS = (N * TOP_K + N_EXPERTS * (MOE_BLOCK - 1) + MOE_BLOCK - 1) // MOE_BLOCK * MOE_BLOCK
N_BLOCKS = N_SLOTS // MOE_BLOCK
HALF = D // 2

LANES = 128
SUBLANES = 8
MXU_N = 256
V7X_VMEM_LIMIT = 60000 * 1024

BF16 = jnp.bfloat16
F32 = jnp.float32
I32 = jnp.int32


def _cparams(sem, vmem_mb):
    return pltpu.CompilerParams(dimension_semantics=sem,
                                vmem_limit_bytes=min(vmem_mb * 1024 * 1024, V7X_VMEM_LIMIT))


ROW_TILES = D // LANES


def _store_row_tiles(ref, vals):
    rows = vals.shape[0]
    for c in range(ROW_TILES):
        ref[pl.ds(c, rows, stride=ROW_TILES), :] = vals[:, c * LANES:(c + 1) * LANES]


def _load_row_tiles(ref, rows):
    return jnp.concatenate([ref[pl.ds(c, rows, stride=ROW_TILES), :] for c in range(ROW_TILES)], axis=1)


def _silu(x):
    return x * jax.nn.sigmoid(x)


ADA_TN = 512


def _ada_kernel(c_ref, w_ref, b_ref, o_ref):
    c = c_ref[...]
    ca = _silu(c).astype(BF16)
    o_ref[...] = jnp.dot(ca, w_ref[...].astype(BF16), preferred_element_type=F32) + b_ref[...]


def _ada(c, w_ada, b_ada):
    width = N_MOD * D
    return pl.pallas_call(
        _ada_kernel,
        out_shape=jax.ShapeDtypeStruct((B, width), F32),
        grid=(width // ADA_TN,),
        in_specs=[pl.BlockSpec((B, D), lambda j: (0, 0)),
                  pl.BlockSpec((D, ADA_TN), lambda j: (0, j)),
                  pl.BlockSpec((1, ADA_TN), lambda j: (0, j))],
        out_specs=pl.BlockSpec((B, ADA_TN), lambda j: (0, j)),
        compiler_params=_cparams(("arbitrary",), 24),
        name="ada",
    )(c, w_ada, b_ada.reshape(1, width))


NORM_TR = 512


def _norm1_kernel(x_ref, w_ref, mod_ref, h_ref):
    x = x_ref[...]
    ms = jnp.mean(x * x, axis=-1, keepdims=True)
    y = x * lax.rsqrt(ms + NORM_EPS) * w_ref[...]
    m = mod_ref[0]
    h_ref[...] = (y * (1.0 + m[1:2]) + m[0:1]).astype(BF16)


def _norm1(x2, norm_w, mod):
    per_b = S // NORM_TR
    return pl.pallas_call(
        _norm1_kernel,
        out_shape=jax.ShapeDtypeStruct((N, D), BF16),
        grid=(N // NORM_TR,),
        in_specs=[pl.BlockSpec((NORM_TR, D), lambda i: (i, 0)),
                  pl.BlockSpec((1, D), lambda i: (0, 0)),
                  pl.BlockSpec((1, N_MOD, D), lambda i: (i // per_b, 0, 0))],
        out_specs=pl.BlockSpec((NORM_TR, D), lambda i: (i, 0)),
        compiler_params=_cparams(("arbitrary",), 32),
        name="norm1",
    )(x2, norm_w.reshape(1, D), mod)


INP_TM = 1024
INP_TN = 512
ROPE_J0 = Q_OFF // INP_TN
ROPE_JK = K_OFF // INP_TN
ROPE_J1 = V_OFF // INP_TN


def _rope_tables():
    pos = np.arange(S, dtype=np.float64)
    inv_freq = 1.0 / (ROPE_THETA ** (np.arange(0, 2 * ROT_HALF, 2, dtype=np.float64) / (2 * ROT_HALF)))
    ang = pos[:, None] * inv_freq[None, :]
    cos, sin = np.cos(ang), np.sin(ang)
    t0 = np.ones((S, HEAD_DIM)); t1 = np.zeros((S, HEAD_DIM)); t2 = np.zeros((S, HEAD_DIM))
    t0[:, :ROT_HALF] = cos; t0[:, ROT_HALF:2 * ROT_HALF] = cos
    t1[:, ROT_HALF:2 * ROT_HALF] = sin
    t2[:, :ROT_HALF] = -sin
    k_tab = np.stack([t0, t1, t2])
    return jnp.asarray(np.stack([k_tab * (ATTN_SCALE * np.log2(np.e)), k_tab]), dtype=F32)


def _inproj_kernel(h_ref, w_ref, tab_ref, o_ref):
    j = pl.program_id(1)
    acc = jnp.dot(h_ref[...], w_ref[...], preferred_element_type=F32)
    is_rope = jnp.logical_and(j >= ROPE_J0, j < ROPE_J1)

    @pl.when(is_rope)
    def _():
        t0 = tab_ref[0, 0]
        t1 = tab_ref[0, 1]
        t2 = tab_ref[0, 2]
        for s in range(INP_TN // HEAD_DIM):
            a = acc[:, s * HEAD_DIM:(s + 1) * HEAD_DIM]
            r = a * t0 + pltpu.roll(a, ROT_HALF, 1) * t1 + pltpu.roll(a, HEAD_DIM - ROT_HALF, 1) * t2
            o_ref[:, s * HEAD_DIM:(s + 1) * HEAD_DIM] = r.astype(BF16)

    @pl.when(jnp.logical_not(is_rope))
    def _():
        o_ref[...] = acc.astype(BF16)


def _inproj(h1, w_in_bf, tabs):
    per_b = S // INP_TM
    return pl.pallas_call(
        _inproj_kernel,
        out_shape=jax.ShapeDtypeStruct((N, IN_WIDTH), BF16),
        grid=(N // INP_TM, IN_WIDTH // INP_TN),
        in_specs=[pl.BlockSpec((INP_TM, D), lambda i, j: (i, 0)),
                  pl.BlockSpec((D, INP_TN), lambda i, j: (0, j)),
                  pl.BlockSpec((1, 3, INP_TM, HEAD_DIM),
                               lambda i, j: (jnp.where(j >= ROPE_JK, 1, 0), 0, i % per_b, 0))],
        out_specs=pl.BlockSpec((INP_TM, INP_TN), lambda i, j: (i, j)),
        compiler_params=_cparams(("arbitrary", "arbitrary"), 48),
        name="inproj",
    )(h1, w_in_bf, tabs)


POOL_PAD = 16


def _pool_kernel(u_ref, w_ref, sc_ref, o_ref, pad_ref):
    g = pl.program_id(1)
    u = u_ref[...].astype(F32)
    pad_ref[0:POOL_PAD, :] = jnp.zeros((POOL_PAD, POOL_DIM), F32)
    sums = []
    s = u
    for sh in (1, 2, 4, 8):
        pad_ref[POOL_PAD:POOL_PAD + S, :] = s
        s = s + pad_ref[POOL_PAD - sh:POOL_PAD - sh + S, :]
        sums.append(s)
    win_sum = jnp.where(g == 0, sums[0], jnp.where(g == 1, sums[1], jnp.where(g == 2, sums[2], sums[3])))
    win = jnp.left_shift(2, g)
    row = lax.broadcasted_iota(I32, (S, POOL_DIM), 0)
    cnt = jnp.minimum(row + 1, win).astype(F32)
    mixed = (win_sum / cnt - u).astype(BF16)
    y = jnp.dot(mixed, w_ref[0].astype(BF16), preferred_element_type=F32) * sc_ref[0]
    o_ref[...] = y.astype(BF16)


def _pool(proj, pool_w, pool_scale):
    return pl.pallas_call(
        _pool_kernel,
        out_shape=jax.ShapeDtypeStruct((N, POOL_WIDTH), BF16),
        grid=(B, POOL_GROUPS),
        in_specs=[pl.BlockSpec((S, POOL_DIM), lambda b, g: (b, g)),
                  pl.BlockSpec((1, POOL_DIM, POOL_DIM), lambda b, g: (g, 0, 0)),
                  pl.BlockSpec((1, 1, POOL_DIM), lambda b, g: (g, 0, 0))],
        out_specs=pl.BlockSpec((S, POOL_DIM), lambda b, g: (b, g)),
        scratch_shapes=[pltpu.VMEM((POOL_PAD + S, POOL_DIM), F32)],
        compiler_params=_cparams(("arbitrary", "arbitrary"), 40),
        name="pool",
    )(proj, pool_w, pool_scale.reshape(POOL_GROUPS, 1, POOL_DIM))


ATT_T = 512


def _attn_kernel(q_ref, k_ref, v_ref, lam_ref, sw_ref, o_ref, s_ref, mx_ref, ls_ref, acc_ref):
    qi = pl.program_id(2)
    q = q_ref[...]
    qs = (q[:, :HEAD_DIM], q[:, HEAD_DIM:])
    mx_ref[...] = jnp.full(mx_ref.shape, -jnp.inf, F32)
    ls_ref[...] = jnp.zeros(ls_ref.shape, F32)
    acc_ref[...] = jnp.zeros(acc_ref.shape, F32)

    def scores(j, masked):
        start = pl.multiple_of(j * ATT_T, ATT_T)
        ks = k_ref[pl.ds(start, ATT_T), :]
        for c in range(2):
            s = lax.dot_general(qs[c], ks[:, c * HEAD_DIM:(c + 1) * HEAD_DIM],
                                (((1,), (1,)), ((), ())), preferred_element_type=F32)
            if masked:
                row = lax.broadcasted_iota(I32, (ATT_T, ATT_T), 0)
                col = lax.broadcasted_iota(I32, (ATT_T, ATT_T), 1)
                s = jnp.where(col <= row, s, -jnp.inf)
            s_ref[c, j] = s
            mx = mx_ref[c]
            for t in range(ATT_T // LANES):
                mx = jnp.maximum(mx, s[:, t * LANES:(t + 1) * LANES])
            mx_ref[c] = mx

    def scores_body(j, carry):
        scores(j, False)
        return carry

    lax.fori_loop(0, qi, scores_body, 0)
    scores(qi, True)

    row_max = [jnp.max(mx_ref[c], axis=-1, keepdims=True) for c in range(2)]

    def pv_body(j, carry):
        start = pl.multiple_of(j * ATT_T, ATT_T)
        vs = v_ref[pl.ds(start, ATT_T), :]
        for c in range(2):
            p = jnp.exp2(s_ref[c, j] - row_max[c])
            ls = ls_ref[c]
            for t in range(ATT_T // LANES):
                ls = ls + p[:, t * LANES:(t + 1) * LANES]
            ls_ref[c] = ls
            acc_ref[c] += jnp.dot(p.astype(BF16), vs, preferred_element_type=F32)
        return carry

    lax.fori_loop(0, qi + 1, pv_body, 0)

    lp = lam_ref[...]
    lam = (jnp.exp(jnp.sum(lp[0:1] * lp[1:2], axis=-1, keepdims=True))
           - jnp.exp(jnp.sum(lp[2:3] * lp[3:4], axis=-1, keepdims=True)) + LAMBDA_INIT)
    l1 = jnp.sum(ls_ref[0], axis=-1, keepdims=True)
    l2 = jnp.sum(ls_ref[1], axis=-1, keepdims=True)
    o = acc_ref[0] / l1 - lam * (acc_ref[1] / l2)
    ms = jnp.mean(o * o, axis=-1, keepdims=True)
    o = o * lax.rsqrt(ms + SUBLN_EPS) * sw_ref[...] * (1.0 - LAMBDA_INIT)
    o_ref[...] = o.astype(BF16)


def _attn(proj, lam_params, subln_w):
    nq = S // ATT_T
    qb, kb, vb = Q_OFF // V_HEAD, K_OFF // V_HEAD, V_OFF // V_HEAD
    return pl.pallas_call(
        _attn_kernel,
        out_shape=jax.ShapeDtypeStruct((N, ATTN_WIDTH), BF16),
        grid=(B, N_HEADS, nq),
        in_specs=[pl.BlockSpec((ATT_T, V_HEAD), lambda b, h, i: (b * nq + i, qb + h)),
                  pl.BlockSpec((S, V_HEAD), lambda b, h, i: (b, kb + h)),
                  pl.BlockSpec((S, V_HEAD), lambda b, h, i: (b, vb + h)),
                  pl.BlockSpec((4, HEAD_DIM), lambda b, h, i: (0, 0)),
                  pl.BlockSpec((1, V_HEAD), lambda b, h, i: (0, 0))],
        out_specs=pl.BlockSpec((ATT_T, V_HEAD), lambda b, h, i: (b * nq + i, h)),
        scratch_shapes=[pltpu.VMEM((2, S // ATT_T, ATT_T, ATT_T), F32),
                        pltpu.VMEM((2, ATT_T, LANES), F32),
                        pltpu.VMEM((2, ATT_T, LANES), F32),
                        pltpu.VMEM((2, ATT_T, V_HEAD), F32)],
        compiler_params=_cparams(("arbitrary", "arbitrary", "arbitrary"), 40),
        name="attn",
    )(proj, proj, proj, lam_params, subln_w.reshape(1, V_HEAD))


MRG_TM = 256


def _merge_kernel(yp_ref, o_ref, gp0_ref, gp1_ref, ga0_ref, ga1_ref, x_ref, mod_ref, w2_ref,
                  wbp_ref, wba_ref, wout_ref, x1_ref, h2t_ref):
    a1 = jnp.dot(yp_ref[...], wbp_ref[...], preferred_element_type=F32)
    a2 = jnp.dot(o_ref[...], wba_ref[...], preferred_element_type=F32)
    m_lo = (jax.nn.sigmoid(gp0_ref[...].astype(F32)) * a1[:, :HALF]
            + jax.nn.sigmoid(ga0_ref[...].astype(F32)) * a2[:, :HALF]).astype(BF16)
    m_hi = (jax.nn.sigmoid(gp1_ref[...].astype(F32)) * a1[:, HALF:]
            + jax.nn.sigmoid(ga1_ref[...].astype(F32)) * a2[:, HALF:]).astype(BF16)
    out = (jnp.dot(m_lo, wout_ref[0:HALF, :], preferred_element_type=F32)
           + jnp.dot(m_hi, wout_ref[HALF:D, :], preferred_element_type=F32))
    m = mod_ref[0]
    x1 = x_ref[...] + m[2:3] * out
    x1_ref[...] = x1
    ms = jnp.mean(x1 * x1, axis=-1, keepdims=True)
    h2 = x1 * lax.rsqrt(ms + NORM_EPS) * w2_ref[...] * (1.0 + m[4:5]) + m[3:4]
    _store_row_tiles(h2t_ref, h2)


def _merge(y_pool, o_attn, proj, x2, mod, norm2_w, wbp, wba, wout):
    per_b = S // MRG_TM
    gpb, gab = GP_OFF // HALF, GA_OFF // HALF
    row = lambda i: (i, 0)
    const = lambda i: (0, 0)
    wspec = lambda shape: pl.BlockSpec(shape, const, pipeline_mode=pl.Buffered(1))
    return pl.pallas_call(
        _merge_kernel,
        out_shape=(jax.ShapeDtypeStruct((N, D), F32),
                   jax.ShapeDtypeStruct((N * ROW_TILES, LANES), F32)),
        grid=(N // MRG_TM,),
        in_specs=[pl.BlockSpec((MRG_TM, POOL_WIDTH), row),
                  pl.BlockSpec((MRG_TM, ATTN_WIDTH), row),
                  pl.BlockSpec((MRG_TM, HALF), lambda i: (i, gpb)),
                  pl.BlockSpec((MRG_TM, HALF), lambda i: (i, gpb + 1)),
                  pl.BlockSpec((MRG_TM, HALF), lambda i: (i, gab)),
                  pl.BlockSpec((MRG_TM, HALF), lambda i: (i, gab + 1)),
                  pl.BlockSpec((MRG_TM, D), row),
                  pl.BlockSpec((1, N_MOD, D), lambda i: (i // per_b, 0, 0)),
                  pl.BlockSpec((1, D), const),
                  wspec((POOL_WIDTH, D)),
                  wspec((ATTN_WIDTH, D)),
                  wspec((D, D))],
        out_specs=(pl.BlockSpec((MRG_TM, D), row),
                   pl.BlockSpec((MRG_TM * ROW_TILES, LANES), row)),
        compiler_params=_cparams(("arbitrary",), 58),
        name="merge",
    )(y_pool, o_attn, proj, proj, proj, proj, x2, mod, norm2_w.reshape(1, D), wbp, wba, wout)


RT_T = 512


def _router_kernel(h_ref, wr_ref, bias_ref, idx_ref, gate_ref, pos_ref, cnt_ref, carry_ref):
    i = pl.program_id(0)

    @pl.when(i == 0)
    def _():
        carry_ref[...] = jnp.zeros(carry_ref.shape, F32)

    logits = lax.dot_general(wr_ref[...], _load_row_tiles(h_ref, RT_T), (((1,), (1,)), ((), ())),
                             precision=lax.Precision.HIGHEST, preferred_element_type=F32)
    scores = jax.nn.sigmoid(logits)
    biased = scores + bias_ref[...]
    neg = -jnp.inf

    b3 = biased.reshape(N_GROUPS, GROUP_SIZE, RT_T)
    io3 = lax.broadcasted_iota(I32, b3.shape, 1)
    m1 = jnp.max(b3, axis=1, keepdims=True)
    i1 = jnp.min(jnp.where(b3 == m1, io3, GROUP_SIZE), axis=1, keepdims=True)
    m2 = jnp.max(jnp.where(io3 == i1, neg, b3), axis=1, keepdims=True)
    gs = m1 + m2

    gio = lax.broadcasted_iota(I32, gs.shape, 0)
    gsel = jnp.zeros(gs.shape, jnp.bool_)
    cur = gs
    for _ in range(TOPK_GROUPS):
        mx = jnp.max(cur, axis=0, keepdims=True)
        ix = jnp.min(jnp.where(cur == mx, gio, N_GROUPS), axis=0, keepdims=True)
        pick = gio == ix
        gsel = jnp.logical_or(gsel, pick)
        cur = jnp.where(pick, neg, cur)

    masked = jnp.where(gsel, b3, neg).reshape(N_EXPERTS, RT_T)
    eio = lax.broadcasted_iota(I32, (N_EXPERTS, RT_T), 0)
    assign = jnp.zeros((N_EXPERTS, RT_T), F32)
    idxs, sels = [], []
    for _ in range(TOP_K):
        mx = jnp.max(masked, axis=0, keepdims=True)
        ix = jnp.min(jnp.where(masked == mx, eio, N_EXPERTS), axis=0, keepdims=True)
        hot = eio == ix
        sels.append(jnp.sum(jnp.where(hot, scores, 0.0), axis=0, keepdims=True))
        idxs.append(ix)
        masked = jnp.where(hot, neg, masked)
        assign = jnp.where(hot, 1.0, assign)

    denom = sels[0]
    for k in range(1, TOP_K):
        denom = denom + sels[k]

    tp = lax.broadcasted_iota(I32, (RT_T, RT_T), 0)
    tc = lax.broadcasted_iota(I32, (RT_T, RT_T), 1)
    before = jnp.where(tp < tc, 1.0, 0.0).astype(BF16)
    rank = jnp.dot(assign.astype(BF16), before, preferred_element_type=F32) + carry_ref[...]
    for k in range(TOP_K):
        idx_ref[k:k + 1, :] = idxs[k]
        gate_ref[k:k + 1, :] = sels[k] / denom * ROUTED_SCALE
        pos_ref[k:k + 1, :] = jnp.sum(jnp.where(eio == idxs[k], rank, 0.0), axis=0,
                                      keepdims=True).astype(I32)
    carry = carry_ref[...] + jnp.sum(assign, axis=1, keepdims=True)
    carry_ref[...] = carry
    cnt_ref[...] = jnp.broadcast_to(carry, cnt_ref.shape)


def _router(h2, w_router_t, bias):
    tok = lambda i: (0, i)
    return pl.pallas_call(
        _router_kernel,
        out_shape=(jax.ShapeDtypeStruct((TOP_K, N), I32),
                   jax.ShapeDtypeStruct((TOP_K, N), F32),
                   jax.ShapeDtypeStruct((TOP_K, N), I32),
                   jax.ShapeDtypeStruct((N_EXPERTS, LANES), F32)),
        grid=(N // RT_T,),
        in_specs=[pl.BlockSpec((RT_T * ROW_TILES, LANES), lambda i: (i, 0)),
                  pl.BlockSpec((N_EXPERTS, D), lambda i: (0, 0)),
                  pl.BlockSpec((N_EXPERTS, 1), lambda i: (0, 0))],
        out_specs=(pl.BlockSpec((TOP_K, RT_T), tok),
                   pl.BlockSpec((TOP_K, RT_T), tok),
                   pl.BlockSpec((TOP_K, RT_T), tok),
                   pl.BlockSpec((N_EXPERTS, LANES), lambda i: (0, 0))),
        scratch_shapes=[pltpu.VMEM((N_EXPERTS, 1), F32)],
        compiler_params=_cparams(("arbitrary",), 32),
        name="router",
    )(h2, w_router_t, bias.reshape(N_EXPERTS, 1))


SLOT_T = 2048


def _slot_kernel(ps_ref, idx_ref, pos_ref, slot_ref):
    idx = idx_ref[...]
    base = jnp.zeros(idx.shape, I32)
    for e in range(N_EXPERTS):
        base = jnp.where(idx == e, ps_ref[e], base)
    slot_ref[...] = base + pos_ref[...]


def _slots(pad_start, idx, pos):
    tok = lambda i, ps: (0, i)
    return pl.pallas_call(
        _slot_kernel,
        out_shape=jax.ShapeDtypeStruct((TOP_K, N), I32),
        grid_spec=pltpu.PrefetchScalarGridSpec(
            num_scalar_prefetch=1, grid=(N // SLOT_T,),
            in_specs=[pl.BlockSpec((TOP_K, SLOT_T), tok), pl.BlockSpec((TOP_K, SLOT_T), tok)],
            out_specs=pl.BlockSpec((TOP_K, SLOT_T), tok)),
        compiler_params=_cparams(("arbitrary",), 16),
        name="slots",
    )(pad_start, idx, pos)


ISSUE_GROUPS = 16


def _start_row_gather(src_hbm, off_ref, n_rows, dst_of_row, sem):
    per_group = n_rows // ISSUE_GROUPS
    state = {"next": 0}

    def start_group():
        g = state["next"]
        state["next"] = g + 1
        for r in range(g * per_group, (g + 1) * per_group):
            src = pl.multiple_of(off_ref[0, 0, r], ROW_TILES)
            pltpu.make_async_copy(src_hbm.at[pl.ds(src, ROW_TILES), :], dst_of_row(r), sem).start(
                priority=r % 2)
    return start_group


def _moe_kernel(be_ref, nu_ref, first_ref, wslot_ref, nxt_ref, tokc_ref, tokn_ref, h_hbm,
                wg_hbm, wu_hbm, wd_hbm, y_ref, buf, sem, wst_g, wst_u, wst_d, wsem, wg_s, wu_s, wd_s, xs, hs):
    b = pl.program_id(0)
    n_used = nu_ref[0]

    def weight_copies(e, ws):
        return (pltpu.make_async_copy(wg_hbm.at[e], wst_g.at[ws], wsem.at[ws, 0]),
                pltpu.make_async_copy(wu_hbm.at[e], wst_u.at[ws], wsem.at[ws, 1]),
                pltpu.make_async_copy(wd_hbm.at[e], wst_d.at[ws], wsem.at[ws, 2]))

    def gather_into(tok_ref, slot):
        return _start_row_gather(h_hbm, tok_ref, MOE_BLOCK,
                                 lambda r: buf.at[slot, r // SUBLANES, :, r % SUBLANES, :], sem.at[slot])

    def wait_gather(slot):
        pltpu.make_async_copy(buf.at[1 - slot], buf.at[slot], sem.at[slot]).wait()

    @pl.when(jnp.logical_and(b == 0, n_used > 0))
    def _():
        for cp in weight_copies(be_ref[0], 0):
            cp.start()

    @pl.when(b == 0)
    def _():
        start_group = gather_into(tokc_ref, 0)
        for _ in range(ISSUE_GROUPS):
            start_group()

    @pl.when(first_ref[b] == 1)
    def _():
        ws = wslot_ref[b]
        for cp in weight_copies(be_ref[b], ws):
            cp.wait()
        wg_s[...] = wst_g[ws].astype(BF16)
        wu_s[...] = wst_u[ws].astype(BF16)
        wd_s[...] = wst_d[ws].astype(BF16)
        nxt = nxt_ref[b]

        @pl.when(nxt >= 0)
        def _():
            for cp in weight_copies(nxt, 1 - ws):
                cp.start()

    @pl.when(b < n_used)
    def _():
        slot = b % 2
        wait_gather(slot)
        start_group = gather_into(tokn_ref, 1 - slot)
        for c in range(ROW_TILES):
            xs[:, c * LANES:(c + 1) * LANES] = buf[slot, :, c].reshape(MOE_BLOCK, LANES).astype(BF16)
            if c % 4 == 3:
                start_group()
        for j in range(EXPERT_DIM // MXU_N):
            cols = slice(j * MXU_N, (j + 1) * MXU_N)
            g = jnp.dot(xs[...], wg_s[:, cols], preferred_element_type=F32)
            start_group()
            u = jnp.dot(xs[...], wu_s[:, cols], preferred_element_type=F32)
            start_group()
            hs[:, cols] = (_silu(g) * u).astype(BF16)
        for n in range(D // MXU_N):
            yv = jnp.dot(hs[...], wd_s[:, n * MXU_N:(n + 1) * MXU_N], preferred_element_type=F32)
            for cc in range(MXU_N // LANES):
                c = n * (MXU_N // LANES) + cc
                y_ref[pl.ds(c, MOE_BLOCK, stride=ROW_TILES), :] = yv[:, cc * LANES:(cc + 1) * LANES]
            start_group()

    @pl.when(b == n_used)
    def _():
        wait_gather(b % 2)

    @pl.when(b >= n_used)
    def _():
        y_ref[...] = jnp.zeros(y_ref.shape, F32)


def _moe(block_expert, n_used, first, wslot, nxt, slot_tok, h2t, wg, wu, wd):
    tok3 = slot_tok.reshape(N_BLOCKS, 1, MOE_BLOCK)
    smem = pltpu.SMEM
    cur = lambda b, *_: (b, 0, 0)
    nxt_blk = lambda b, *_: (jnp.minimum(b + 1, N_BLOCKS - 1), 0, 0)
    any_spec = pl.BlockSpec(memory_space=pl.ANY)
    return pl.pallas_call(
        _moe_kernel,
        out_shape=jax.ShapeDtypeStruct((N_SLOTS * ROW_TILES, LANES), F32),
        grid_spec=pltpu.PrefetchScalarGridSpec(
            num_scalar_prefetch=5, grid=(N_BLOCKS,),
            in_specs=[pl.BlockSpec((1, 1, MOE_BLOCK), cur, memory_space=smem),
                      pl.BlockSpec((1, 1, MOE_BLOCK), nxt_blk, memory_space=smem),
                      any_spec, any_spec, any_spec, any_spec],
            out_specs=pl.BlockSpec((MOE_BLOCK * ROW_TILES, LANES), lambda b, *_: (b, 0)),
            scratch_shapes=[pltpu.VMEM((2, MOE_BLOCK // SUBLANES, ROW_TILES, SUBLANES, LANES), F32),
                            pltpu.SemaphoreType.DMA((2,)),
                            pltpu.VMEM((2, D, EXPERT_DIM), F32),
                            pltpu.VMEM((2, D, EXPERT_DIM), F32),
                            pltpu.VMEM((2, EXPERT_DIM, D), F32),
                            pltpu.SemaphoreType.DMA((2, 3)),
                            pltpu.VMEM((D, EXPERT_DIM), BF16),
                            pltpu.VMEM((D, EXPERT_DIM), BF16),
                            pltpu.VMEM((EXPERT_DIM, D), BF16),
                            pltpu.VMEM((MOE_BLOCK, D), BF16),
                            pltpu.VMEM((MOE_BLOCK, EXPERT_DIM), BF16)]),
        compiler_params=_cparams(("arbitrary",), 58),
        name="moe",
    )(block_expert, n_used, first, wslot, nxt, tok3, tok3, h2t, wg, wu, wd)


FIN_T = 128


def _final_kernel(slc_ref, sln_ref, y_hbm, gt_ref, x1_ref, h2_ref, mod_ref, wsg_ref, wsu_ref, wsd_ref,
                  wf_ref, o_ref, buf, sem, hs, x2_s):
    i = pl.program_id(0)
    n_steps = pl.num_programs(0)

    def gather_into(sl_ref, slot):
        def dst(q):
            t, k = q // TOP_K, q % TOP_K
            return buf.at[slot, k, t // SUBLANES, :, t % SUBLANES, :]
        return _start_row_gather(y_hbm, sl_ref, FIN_T * TOP_K, dst, sem.at[slot])

    def wait_gather(slot):
        pltpu.make_async_copy(buf.at[1 - slot], buf.at[slot], sem.at[slot]).wait()

    @pl.when(i == 0)
    def _():
        start_group = gather_into(slc_ref, 0)
        for _ in range(ISSUE_GROUPS):
            start_group()

    slot = i % 2
    start_group = gather_into(sln_ref, 1 - slot)

    h = _load_row_tiles(h2_ref, FIN_T).astype(BF16)
    for j in range(SHARED_DIM // MXU_N):
        cols = slice(j * MXU_N, (j + 1) * MXU_N)
        sg = jnp.dot(h, wsg_ref[:, cols], preferred_element_type=F32)
        start_group()
        su = jnp.dot(h, wsu_ref[:, cols], preferred_element_type=F32)
        start_group()
        hs[:, cols] = (_silu(sg) * su).astype(BF16)

    wait_gather(slot)
    gt = gt_ref[...]
    m = mod_ref[0]
    ssq = jnp.zeros((FIN_T, 1), F32)
    for n in range(D // MXU_N):
        cols = slice(n * MXU_N, (n + 1) * MXU_N)
        acc = jnp.dot(hs[...], wsd_ref[:, cols], preferred_element_type=F32)
        for k in range(TOP_K):
            yk = jnp.concatenate(
                [buf[slot, k, :, n * (MXU_N // LANES) + cc].reshape(FIN_T, LANES) for cc in range(MXU_N // LANES)],
                axis=1)
            acc = acc + gt[:, k:k + 1] * yk
        x2 = x1_ref[:, cols] + m[5:6, cols] * acc
        x2_s[:, cols] = x2
        ssq = ssq + jnp.sum(x2 * x2, axis=-1, keepdims=True)
        start_group()
        if n % 2 == 1:
            start_group()
    o_ref[...] = x2_s[...] * lax.rsqrt(ssq * (1.0 / D) + NORM_EPS) * wf_ref[...]

    @pl.when(i == n_steps - 1)
    def _():
        wait_gather(1 - slot)


def _final(slot_tk, y_t, gates_t, x1, h2t, mod, wsg, wsu, wsd, wf):
    n_steps = N // FIN_T
    per_b = S // FIN_T
    sl3 = slot_tk.reshape(n_steps, 1, FIN_T * TOP_K)
    smem = pltpu.SMEM
    row = lambda i: (i, 0)
    const = lambda i: (0, 0)
    return pl.pallas_call(
        _final_kernel,
        out_shape=jax.ShapeDtypeStruct((N, D), F32),
        grid=(n_steps,),
        in_specs=[pl.BlockSpec((1, 1, FIN_T * TOP_K), lambda i: (i, 0, 0), memory_space=smem),
                  pl.BlockSpec((1, 1, FIN_T * TOP_K),
                               lambda i: (jnp.minimum(i + 1, n_steps - 1), 0, 0), memory_space=smem),
                  pl.BlockSpec(memory_space=pl.ANY),
                  pl.BlockSpec((FIN_T, TOP_K), row),
                  pl.BlockSpec((FIN_T, D), row),
                  pl.BlockSpec((FIN_T * ROW_TILES, LANES), row),
                  pl.BlockSpec((1, N_MOD, D), lambda i: (i // per_b, 0, 0)),
                  pl.BlockSpec((D, SHARED_DIM), const),
                  pl.BlockSpec((D, SHARED_DIM), const),
                  pl.BlockSpec((SHARED_DIM, D), const),
                  pl.BlockSpec((1, D), const)],
        out_specs=pl.BlockSpec((FIN_T, D), row),
        scratch_shapes=[pltpu.VMEM((2, TOP_K, FIN_T // SUBLANES, ROW_TILES, SUBLANES, LANES), F32),
                        pltpu.SemaphoreType.DMA((2,)),
                        pltpu.VMEM((FIN_T, SHARED_DIM), BF16),
                        pltpu.VMEM((FIN_T, D), F32)],
        compiler_params=_cparams(("arbitrary",), 48),
        name="final",
    )(sl3, sl3, y_t, gates_t, x1, h2t, mod, wsg, wsu, wsd, wf.reshape(1, D))


def kernel(x, c, w_ada, b_ada, norm1_w, w_in, pool_w, pool_scale, w_branch_pool, lambda_q1, lambda_k1,
           lambda_q2, lambda_k2, subln_w, w_branch_attn, w_out, norm2_w, w_router, router_bias,
           w_gate_e, w_up_e, w_down_e, w_sh_gate, w_sh_up, w_sh_down, final_norm_w):
    assert x.shape == (B, S, D) and w_ada.shape[0] == 1
    x2 = x.reshape(N, D)
    mod = _ada(c, w_ada[0], b_ada[0]).reshape(B, N_MOD, D)

    h1 = _norm1(x2, norm1_w[0], mod)
    proj = _inproj(h1, w_in[0].astype(BF16), _rope_tables())
    y_pool = _pool(proj, pool_w[0], pool_scale[0])
    lam_params = jnp.stack([lambda_q1[0], lambda_k1[0], lambda_q2[0], lambda_k2[0]])
    o_attn = _attn(proj, lam_params, subln_w[0])
    x1, h2t = _merge(y_pool, o_attn, proj, x2, mod, norm2_w[0],
                     w_branch_pool[0].astype(BF16), w_branch_attn[0].astype(BF16),
                     w_out[0].astype(BF16))

    idx, gates, pos, cnt = _router(h2t, w_router[0].T, router_bias[0])
    counts = cnt[:, 0].astype(I32)
    padded = (counts + MOE_BLOCK - 1) // MOE_BLOCK * MOE_BLOCK
    pad_end = jnp.cumsum(padded)
    pad_start = pad_end - padded
    n_used = (pad_end[-1:] // MOE_BLOCK).astype(I32)
    block_row0 = jnp.arange(N_BLOCKS, dtype=I32) * MOE_BLOCK
    block_expert = jnp.minimum(jnp.sum((pad_end[None, :] <= block_row0[:, None]).astype(I32), axis=1),
                               N_EXPERTS - 1).astype(I32)
    slot = _slots(pad_start.astype(I32), idx, pos)
    tok_rows = jnp.broadcast_to(jnp.arange(N, dtype=I32)[None, :] * ROW_TILES, (TOP_K, N))
    slot_tok = jnp.zeros((N_SLOTS,), I32).at[slot.reshape(-1)].set(
        tok_rows.reshape(-1), unique_indices=True, mode="promise_in_bounds")

    bidx = jnp.arange(N_BLOCKS, dtype=I32)
    prev_expert = jnp.concatenate([jnp.full((1,), -1, I32), block_expert[:-1]])
    first = jnp.logical_and(bidx < n_used[0], block_expert != prev_expert)
    wslot = ((jnp.cumsum(first.astype(I32)) - 1) % 2).astype(I32)
    later_first = lax.cummin(jnp.where(first, bidx, N_BLOCKS)[::-1])[::-1]
    next_first = jnp.concatenate([later_first[1:], jnp.full((1,), N_BLOCKS, I32)])
    nxt = jnp.where(next_first < N_BLOCKS, block_expert[jnp.minimum(next_first, N_BLOCKS - 1)], -1).astype(I32)

    y_t = _moe(block_expert, n_used, first.astype(I32), wslot, nxt, slot_tok, h2t,
               w_gate_e[0], w_up_e[0], w_down_e[0])
    out = _final((slot.T * ROW_TILES).reshape(-1), y_t, gates.T, x1, h2t, mod,
                 w_sh_gate[0].astype(BF16), w_sh_up[0].astype(BF16), w_sh_down[0].astype(BF16),
                 final_norm_w)
    return out.reshape(B, S, D)
```

```python
import numpy as np
import jax
import jax.numpy as jnp
from jax import lax
from jax.experimental import pallas as pl
from jax.experimental.pallas import tpu as pltpu

D = 2048
B = 8
S = 2048
N = B * S
POOL_GROUPS = 4
POOL_DIM = 256
POOL_WIDTH = POOL_GROUPS * POOL_DIM
N_HEADS = 8
HEAD_DIM = 128
V_HEAD = 2 * HEAD_DIM
QK_WIDTH = N_HEADS * 2 * HEAD_DIM
ATTN_WIDTH = N_HEADS * V_HEAD
ROT_HALF = HEAD_DIM // 8
ROPE_THETA = 500000.0
ATTN_SCALE = HEAD_DIM ** -0.5
SUBLN_EPS = 1e-5
NORM_EPS = 1e-6
LAMBDA_INIT = 0.8 - 0.6 * float(np.exp(-0.3 * 0))
IN_WIDTH = POOL_WIDTH + 2 * QK_WIDTH + ATTN_WIDTH + 2 * D
Q_OFF = POOL_WIDTH
K_OFF = Q_OFF + QK_WIDTH
V_OFF = K_OFF + QK_WIDTH
GP_OFF = V_OFF + ATTN_WIDTH
GA_OFF = GP_OFF + D
N_EXPERTS = 64
EXPERT_DIM = 512
SHARED_DIM = 512
TOP_K = 8
N_GROUPS = 8
TOPK_GROUPS = 4
GROUP_SIZE = N_EXPERTS // N_GROUPS
ROUTED_SCALE = 2.5
MOE_BLOCK = 512
N_MOD = 6
N_SLOTS = (N * TOP_K + N_EXPERTS * (MOE_BLOCK - 1) + MOE_BLOCK - 1) // MOE_BLOCK * MOE_BLOCK
N_BLOCKS = N_SLOTS // MOE_BLOCK
HALF = D // 2

LANES = 128
SUBLANES = 8
MXU_N = 256
V7X_VMEM_LIMIT = 60000 * 1024

BF16 = jnp.bfloat16
F32 = jnp.float32
I32 = jnp.int32


def _cparams(sem, vmem_mb):
    return pltpu.CompilerParams(dimension_semantics=sem,
                                vmem_limit_bytes=min(vmem_mb * 1024 * 1024, V7X_VMEM_LIMIT))


ROW_TILES = D // LANES


def _store_row_tiles(ref, vals):
    rows = vals.shape[0]
    for c in range(ROW_TILES):
        ref[pl.ds(c, rows, stride=ROW_TILES), :] = vals[:, c * LANES:(c + 1) * LANES]


def _load_row_tiles(ref, rows):
    return jnp.concatenate([ref[pl.ds(c, rows, stride=ROW_TILES), :] for c in range(ROW_TILES)], axis=1)


def _silu(x):
    return x * jax.nn.sigmoid(x)


ADA_TN = 512


def _ada_kernel(c_ref, w_ref, b_ref, o_ref):
    c = c_ref[...]
    ca = _silu(c).astype(BF16)
    o_ref[...] = jnp.dot(ca, w_ref[...].astype(BF16), preferred_element_type=F32) + b_ref[...]


def _ada(c, w_ada, b_ada):
    width = N_MOD * D
    return pl.pallas_call(
        _ada_kernel,
        out_shape=jax.ShapeDtypeStruct((B, width), F32),
        grid=(width // ADA_TN,),
        in_specs=[pl.BlockSpec((B, D), lambda j: (0, 0)),
                  pl.BlockSpec((D, ADA_TN), lambda j: (0, j)),
                  pl.BlockSpec((1, ADA_TN), lambda j: (0, j))],
        out_specs=pl.BlockSpec((B, ADA_TN), lambda j: (0, j)),
        compiler_params=_cparams(("arbitrary",), 24),
        name="ada",
    )(c, w_ada, b_ada.reshape(1, width))


NORM_TR = 512


def _norm1_kernel(x_ref, w_ref, mod_ref, h_ref):
    x = x_ref[...]
    ms = jnp.mean(x * x, axis=-1, keepdims=True)
    y = x * lax.rsqrt(ms + NORM_EPS) * w_ref[...]
    m = mod_ref[0]
    h_ref[...] = (y * (1.0 + m[1:2]) + m[0:1]).astype(BF16)


def _norm1(x2, norm_w, mod):
    per_b = S // NORM_TR
    return pl.pallas_call(
        _norm1_kernel,
        out_shape=jax.ShapeDtypeStruct((N, D), BF16),
        grid=(N // NORM_TR,),
        in_specs=[pl.BlockSpec((NORM_TR, D), lambda i: (i, 0)),
                  pl.BlockSpec((1, D), lambda i: (0, 0)),
                  pl.BlockSpec((1, N_MOD, D), lambda i: (i // per_b, 0, 0))],
        out_specs=pl.BlockSpec((NORM_TR, D), lambda i: (i, 0)),
        compiler_params=_cparams(("arbitrary",), 32),
        name="norm1",
    )(x2, norm_w.reshape(1, D), mod)


INP_TM = 1024
INP_TN = 1024
ROPE_J0 = Q_OFF // INP_TN
ROPE_JK = K_OFF // INP_TN
ROPE_J1 = V_OFF // INP_TN


def _rope_tables():
    pos = np.arange(S, dtype=np.float64)
    inv_freq = 1.0 / (ROPE_THETA ** (np.arange(0, 2 * ROT_HALF, 2, dtype=np.float64) / (2 * ROT_HALF)))
    ang = pos[:, None] * inv_freq[None, :]
    cos, sin = np.cos(ang), np.sin(ang)
    t0 = np.ones((S, HEAD_DIM)); t1 = np.zeros((S, HEAD_DIM)); t2 = np.zeros((S, HEAD_DIM))
    t0[:, :ROT_HALF] = cos; t0[:, ROT_HALF:2 * ROT_HALF] = cos
    t1[:, ROT_HALF:2 * ROT_HALF] = sin
    t2[:, :ROT_HALF] = -sin
    k_tab = np.stack([t0, t1, t2])
    return jnp.asarray(np.stack([k_tab * (ATTN_SCALE * np.log2(np.e)), k_tab]), dtype=F32)


def _inproj_kernel(h_ref, w_ref, tab_ref, o_ref):
    j = pl.program_id(1)
    acc = jnp.dot(h_ref[...], w_ref[...], preferred_element_type=F32)
    is_rope = jnp.logical_and(j >= ROPE_J0, j < ROPE_J1)

    @pl.when(is_rope)
    def _():
        t0 = tab_ref[0, 0]
        t1 = tab_ref[0, 1]
        t2 = tab_ref[0, 2]
        for s in range(INP_TN // HEAD_DIM):
            a = acc[:, s * HEAD_DIM:(s + 1) * HEAD_DIM]
            r = a * t0 + pltpu.roll(a, ROT_HALF, 1) * t1 + pltpu.roll(a, HEAD_DIM - ROT_HALF, 1) * t2
            o_ref[:, s * HEAD_DIM:(s + 1) * HEAD_DIM] = r.astype(BF16)

    @pl.when(jnp.logical_not(is_rope))
    def _():
        o_ref[...] = acc.astype(BF16)


def _inproj(h1, w_in_bf, tabs):
    per_b = S // INP_TM
    return pl.pallas_call(
        _inproj_kernel,
        out_shape=jax.ShapeDtypeStruct((N, IN_WIDTH), BF16),
        grid=(N // INP_TM, IN_WIDTH // INP_TN),
        in_specs=[pl.BlockSpec((INP_TM, D), lambda i, j: (i, 0)),
                  pl.BlockSpec((D, INP_TN), lambda i, j: (0, j)),
                  pl.BlockSpec((1, 3, INP_TM, HEAD_DIM),
                               lambda i, j: (jnp.where(j >= ROPE_JK, 1, 0), 0, i % per_b, 0))],
        out_specs=pl.BlockSpec((INP_TM, INP_TN), lambda i, j: (i, j)),
        compiler_params=_cparams(("arbitrary", "arbitrary"), 48),
        name="inproj",
    )(h1, w_in_bf, tabs)


POOL_PAD = 16


def _pool_kernel(u_ref, w_ref, sc_ref, o_ref, pad_ref):
    g = pl.program_id(1)
    u = u_ref[...].astype(F32)
    pad_ref[0:POOL_PAD, :] = jnp.zeros((POOL_PAD, POOL_DIM), F32)
    sums = []
    s = u
    for sh in (1, 2, 4, 8):
        pad_ref[POOL_PAD:POOL_PAD + S, :] = s
        s = s + pad_ref[POOL_PAD - sh:POOL_PAD - sh + S, :]
        sums.append(s)
    win_sum = jnp.where(g == 0, sums[0], jnp.where(g == 1, sums[1], jnp.where(g == 2, sums[2], sums[3])))
    win = jnp.left_shift(2, g)
    row = lax.broadcasted_iota(I32, (S, POOL_DIM), 0)
    cnt = jnp.minimum(row + 1, win).astype(F32)
    mixed = (win_sum / cnt - u).astype(BF16)
    y = jnp.dot(mixed, w_ref[0].astype(BF16), preferred_element_type=F32) * sc_ref[0]
    o_ref[...] = y.astype(BF16)


def _pool(proj, pool_w, pool_scale):
    return pl.pallas_call(
        _pool_kernel,
        out_shape=jax.ShapeDtypeStruct((N, POOL_WIDTH), BF16),
        grid=(B, POOL_GROUPS),
        in_specs=[pl.BlockSpec((S, POOL_DIM), lambda b, g: (b, g)),
                  pl.BlockSpec((1, POOL_DIM, POOL_DIM), lambda b, g: (g, 0, 0)),
                  pl.BlockSpec((1, 1, POOL_DIM), lambda b, g: (g, 0, 0))],
        out_specs=pl.BlockSpec((S, POOL_DIM), lambda b, g: (b, g)),
        scratch_shapes=[pltpu.VMEM((POOL_PAD + S, POOL_DIM), F32)],
        compiler_params=_cparams(("arbitrary", "arbitrary"), 40),
        name="pool",
    )(proj, pool_w, pool_scale.reshape(POOL_GROUPS, 1, POOL_DIM))


ATT_T = 512


def _attn_kernel(q_ref, k_ref, v_ref, lam_ref, sw_ref, o_ref, s_ref, mx_ref, ls_ref, acc_ref):
    qi = pl.program_id(2)
    q = q_ref[...]
    qs = (q[:, :HEAD_DIM], q[:, HEAD_DIM:])
    mx_ref[...] = jnp.full(mx_ref.shape, -jnp.inf, F32)
    ls_ref[...] = jnp.zeros(ls_ref.shape, F32)
    acc_ref[...] = jnp.zeros(acc_ref.shape, F32)

    def scores(j, masked):
        start = pl.multiple_of(j * ATT_T, ATT_T)
        ks = k_ref[pl.ds(start, ATT_T), :]
        for c in range(2):
            s = lax.dot_general(qs[c], ks[:, c * HEAD_DIM:(c + 1) * HEAD_DIM],
                                (((1,), (1,)), ((), ())), preferred_element_type=F32)
            if masked:
                row = lax.broadcasted_iota(I32, (ATT_T, ATT_T), 0)
                col = lax.broadcasted_iota(I32, (ATT_T, ATT_T), 1)
                s = jnp.where(col <= row, s, -jnp.inf)
            s_ref[c, j] = s
            mx = mx_ref[c]
            for t in range(ATT_T // LANES):
                mx = jnp.maximum(mx, s[:, t * LANES:(t + 1) * LANES])
            mx_ref[c] = mx

    def scores_body(j, carry):
        scores(j, False)
        return carry

    lax.fori_loop(0, qi, scores_body, 0)
    scores(qi, True)

    row_max = [jnp.max(mx_ref[c], axis=-1, keepdims=True) for c in range(2)]

    def pv_body(j, carry):
        start = pl.multiple_of(j * ATT_T, ATT_T)
        vs = v_ref[pl.ds(start, ATT_T), :]
        for c in range(2):
            p = jnp.exp2(s_ref[c, j] - row_max[c])
            ls = ls_ref[c]
            for t in range(ATT_T // LANES):
                ls = ls + p[:, t * LANES:(t + 1) * LANES]
            ls_ref[c] = ls
            acc_ref[c] += jnp.dot(p.astype(BF16), vs, preferred_element_type=F32)
        return carry

    lax.fori_loop(0, qi + 1, pv_body, 0)

    lp = lam_ref[...]
    lam = (jnp.exp(jnp.sum(lp[0:1] * lp[1:2], axis=-1, keepdims=True))
           - jnp.exp(jnp.sum(lp[2:3] * lp[3:4], axis=-1, keepdims=True)) + LAMBDA_INIT)
    l1 = jnp.sum(ls_ref[0], axis=-1, keepdims=True)
    l2 = jnp.sum(ls_ref[1], axis=-1, keepdims=True)
    o = acc_ref[0] / l1 - lam * (acc_ref[1] / l2)
    ms = jnp.mean(o * o, axis=-1, keepdims=True)
    o = o * lax.rsqrt(ms + SUBLN_EPS) * sw_ref[...] * (1.0 - LAMBDA_INIT)
    o_ref[...] = o.astype(BF16)


def _attn(proj, lam_params, subln_w):
    nq = S // ATT_T
    qb, kb, vb = Q_OFF // V_HEAD, K_OFF // V_HEAD, V_OFF // V_HEAD
    return pl.pallas_call(
        _attn_kernel,
        out_shape=jax.ShapeDtypeStruct((N, ATTN_WIDTH), BF16),
        grid=(B, N_HEADS, nq),
        in_specs=[pl.BlockSpec((ATT_T, V_HEAD), lambda b, h, i: (b * nq + i, qb + h)),
                  pl.BlockSpec((S, V_HEAD), lambda b, h, i: (b, kb + h)),
                  pl.BlockSpec((S, V_HEAD), lambda b, h, i: (b, vb + h)),
                  pl.BlockSpec((4, HEAD_DIM), lambda b, h, i: (0, 0)),
                  pl.BlockSpec((1, V_HEAD), lambda b, h, i: (0, 0))],
        out_specs=pl.BlockSpec((ATT_T, V_HEAD), lambda b, h, i: (b * nq + i, h)),
        scratch_shapes=[pltpu.VMEM((2, S // ATT_T, ATT_T, ATT_T), F32),
                        pltpu.VMEM((2, ATT_T, LANES), F32),
                        pltpu.VMEM((2, ATT_T, LANES), F32),
                        pltpu.VMEM((2, ATT_T, V_HEAD), F32)],
        compiler_params=_cparams(("arbitrary", "arbitrary", "arbitrary"), 40),
        name="attn",
    )(proj, proj, proj, lam_params, subln_w.reshape(1, V_HEAD))


MRG_TM = 256


def _merge_kernel(yp_ref, o_ref, gp0_ref, gp1_ref, ga0_ref, ga1_ref, x_ref, mod_ref, w2_ref,
                  wbp_ref, wba_ref, wout_ref, x1_ref, h2t_ref):
    a1 = jnp.dot(yp_ref[...], wbp_ref[...], preferred_element_type=F32)
    a2 = jnp.dot(o_ref[...], wba_ref[...], preferred_element_type=F32)
    m_lo = (jax.nn.sigmoid(gp0_ref[...].astype(F32)) * a1[:, :HALF]
            + jax.nn.sigmoid(ga0_ref[...].astype(F32)) * a2[:, :HALF]).astype(BF16)
    m_hi = (jax.nn.sigmoid(gp1_ref[...].astype(F32)) * a1[:, HALF:]
            + jax.nn.sigmoid(ga1_ref[...].astype(F32)) * a2[:, HALF:]).astype(BF16)
    out = (jnp.dot(m_lo, wout_ref[0:HALF, :], preferred_element_type=F32)
           + jnp.dot(m_hi, wout_ref[HALF:D, :], preferred_element_type=F32))
    m = mod_ref[0]
    x1 = x_ref[...] + m[2:3] * out
    x1_ref[...] = x1
    ms = jnp.mean(x1 * x1, axis=-1, keepdims=True)
    h2 = x1 * lax.rsqrt(ms + NORM_EPS) * w2_ref[...] * (1.0 + m[4:5]) + m[3:4]
    _store_row_tiles(h2t_ref, h2)


def _merge(y_pool, o_attn, proj, x2, mod, norm2_w, wbp, wba, wout):
    per_b = S // MRG_TM
    gpb, gab = GP_OFF // HALF, GA_OFF // HALF
    row = lambda i: (i, 0)
    const = lambda i: (0, 0)
    wspec = lambda shape: pl.BlockSpec(shape, const, pipeline_mode=pl.Buffered(1))
    return pl.pallas_call(
        _merge_kernel,
        out_shape=(jax.ShapeDtypeStruct((N, D), F32),
                   jax.ShapeDtypeStruct((N * ROW_TILES, LANES), F32)),
        grid=(N // MRG_TM,),
        in_specs=[pl.BlockSpec((MRG_TM, POOL_WIDTH), row),
                  pl.BlockSpec((MRG_TM, ATTN_WIDTH), row),
                  pl.BlockSpec((MRG_TM, HALF), lambda i: (i, gpb)),
                  pl.BlockSpec((MRG_TM, HALF), lambda i: (i, gpb + 1)),
                  pl.BlockSpec((MRG_TM, HALF), lambda i: (i, gab)),
                  pl.BlockSpec((MRG_TM, HALF), lambda i: (i, gab + 1)),
                  pl.BlockSpec((MRG_TM, D), row),
                  pl.BlockSpec((1, N_MOD, D), lambda i: (i // per_b, 0, 0)),
                  pl.BlockSpec((1, D), const),
                  wspec((POOL_WIDTH, D)),
                  wspec((ATTN_WIDTH, D)),
                  wspec((D, D))],
        out_specs=(pl.BlockSpec((MRG_TM, D), row),
                   pl.BlockSpec((MRG_TM * ROW_TILES, LANES), row)),
        compiler_params=_cparams(("arbitrary",), 58),
        name="merge",
    )(y_pool, o_attn, proj, proj, proj, proj, x2, mod, norm2_w.reshape(1, D), wbp, wba, wout)


RT_T = 512


def _router_kernel(h_ref, wr_ref, bias_ref, idx_ref, gate_ref, pos_ref, cnt_ref, carry_ref):
    i = pl.program_id(0)

    @pl.when(i == 0)
    def _():
        carry_ref[...] = jnp.zeros(carry_ref.shape, F32)

    logits = lax.dot_general(wr_ref[...], _load_row_tiles(h_ref, RT_T), (((1,), (1,)), ((), ())),
                             precision=lax.Precision.HIGHEST, preferred_element_type=F32)
    scores = jax.nn.sigmoid(logits)
    biased = scores + bias_ref[...]
    neg = -jnp.inf

    b3 = biased.reshape(N_GROUPS, GROUP_SIZE, RT_T)
    io3 = lax.broadcasted_iota(I32, b3.shape, 1)
    m1 = jnp.max(b3, axis=1, keepdims=True)
    i1 = jnp.min(jnp.where(b3 == m1, io3, GROUP_SIZE), axis=1, keepdims=True)
    m2 = jnp.max(jnp.where(io3 == i1, neg, b3), axis=1, keepdims=True)
    gs = m1 + m2

    gio = lax.broadcasted_iota(I32, gs.shape, 0)
    gsel = jnp.zeros(gs.shape, jnp.bool_)
    cur = gs
    for _ in range(TOPK_GROUPS):
        mx = jnp.max(cur, axis=0, keepdims=True)
        ix = jnp.min(jnp.where(cur == mx, gio, N_GROUPS), axis=0, keepdims=True)
        pick = gio == ix
        gsel = jnp.logical_or(gsel, pick)
        cur = jnp.where(pick, neg, cur)

    masked = jnp.where(gsel, b3, neg).reshape(N_EXPERTS, RT_T)
    eio = lax.broadcasted_iota(I32, (N_EXPERTS, RT_T), 0)
    assign = jnp.zeros((N_EXPERTS, RT_T), F32)
    idxs, sels = [], []
    for _ in range(TOP_K):
        mx = jnp.max(masked, axis=0, keepdims=True)
        ix = jnp.min(jnp.where(masked == mx, eio, N_EXPERTS), axis=0, keepdims=True)
        hot = eio == ix
        sels.append(jnp.sum(jnp.where(hot, scores, 0.0), axis=0, keepdims=True))
        idxs.append(ix)
        masked = jnp.where(hot, neg, masked)
        assign = jnp.where(hot, 1.0, assign)

    denom = sels[0]
    for k in range(1, TOP_K):
        denom = denom + sels[k]

    tp = lax.broadcasted_iota(I32, (RT_T, RT_T), 0)
    tc = lax.broadcasted_iota(I32, (RT_T, RT_T), 1)
    before = jnp.where(tp < tc, 1.0, 0.0).astype(BF16)
    rank = jnp.dot(assign.astype(BF16), before, preferred_element_type=F32) + carry_ref[...]
    for k in range(TOP_K):
        idx_ref[k:k + 1, :] = idxs[k]
        gate_ref[k:k + 1, :] = sels[k] / denom * ROUTED_SCALE
        pos_ref[k:k + 1, :] = jnp.sum(jnp.where(eio == idxs[k], rank, 0.0), axis=0,
                                      keepdims=True).astype(I32)
    carry = carry_ref[...] + jnp.sum(assign, axis=1, keepdims=True)
    carry_ref[...] = carry
    cnt_ref[...] = jnp.broadcast_to(carry, cnt_ref.shape)


def _router(h2, w_router_t, bias):
    tok = lambda i: (0, i)
    return pl.pallas_call(
        _router_kernel,
        out_shape=(jax.ShapeDtypeStruct((TOP_K, N), I32),
                   jax.ShapeDtypeStruct((TOP_K, N), F32),
                   jax.ShapeDtypeStruct((TOP_K, N), I32),
                   jax.ShapeDtypeStruct((N_EXPERTS, LANES), F32)),
        grid=(N // RT_T,),
        in_specs=[pl.BlockSpec((RT_T * ROW_TILES, LANES), lambda i: (i, 0)),
                  pl.BlockSpec((N_EXPERTS, D), lambda i: (0, 0)),
                  pl.BlockSpec((N_EXPERTS, 1), lambda i: (0, 0))],
        out_specs=(pl.BlockSpec((TOP_K, RT_T), tok),
                   pl.BlockSpec((TOP_K, RT_T), tok),
                   pl.BlockSpec((TOP_K, RT_T), tok),
                   pl.BlockSpec((N_EXPERTS, LANES), lambda i: (0, 0))),
        scratch_shapes=[pltpu.VMEM((N_EXPERTS, 1), F32)],
        compiler_params=_cparams(("arbitrary",), 32),
        name="router",
    )(h2, w_router_t, bias.reshape(N_EXPERTS, 1))


SLOT_T = 2048


def _slot_kernel(ps_ref, idx_ref, pos_ref, slot_ref):
    idx = idx_ref[...]
    base = jnp.zeros(idx.shape, I32)
    for e in range(N_EXPERTS):
        base = jnp.where(idx == e, ps_ref[e], base)
    slot_ref[...] = base + pos_ref[...]


def _slots(pad_start, idx, pos):
    tok = lambda i, ps: (0, i)
    return pl.pallas_call(
        _slot_kernel,
        out_shape=jax.ShapeDtypeStruct((TOP_K, N), I32),
        grid_spec=pltpu.PrefetchScalarGridSpec(
            num_scalar_prefetch=1, grid=(N // SLOT_T,),
            in_specs=[pl.BlockSpec((TOP_K, SLOT_T), tok), pl.BlockSpec((TOP_K, SLOT_T), tok)],
            out_specs=pl.BlockSpec((TOP_K, SLOT_T), tok)),
        compiler_params=_cparams(("arbitrary",), 16),
        name="slots",
    )(pad_start, idx, pos)


ISSUE_GROUPS = 16


def _start_row_gather(src_hbm, off_ref, n_rows, dst_of_row, sem):
    per_group = n_rows // ISSUE_GROUPS
    state = {"next": 0}

    def start_group():
        g = state["next"]
        state["next"] = g + 1
        for r in range(g * per_group, (g + 1) * per_group):
            src = pl.multiple_of(off_ref[0, 0, r], ROW_TILES)
            pltpu.make_async_copy(src_hbm.at[pl.ds(src, ROW_TILES), :], dst_of_row(r), sem).start(
                priority=r % 2)
    return start_group


def _moe_kernel(be_ref, nu_ref, first_ref, wslot_ref, nxt_ref, tokc_ref, tokn_ref, h_hbm,
                wg_hbm, wu_hbm, wd_hbm, y_ref, buf, sem, wst_g, wst_u, wst_d, wsem, wg_s, wu_s, wd_s, xs, hs):
    b = pl.program_id(0)
    n_used = nu_ref[0]

    def weight_copies(e, ws):
        return (pltpu.make_async_copy(wg_hbm.at[e], wst_g.at[ws], wsem.at[ws, 0]),
                pltpu.make_async_copy(wu_hbm.at[e], wst_u.at[ws], wsem.at[ws, 1]),
                pltpu.make_async_copy(wd_hbm.at[e], wst_d.at[ws], wsem.at[ws, 2]))

    def gather_into(tok_ref, slot):
        return _start_row_gather(h_hbm, tok_ref, MOE_BLOCK,
                                 lambda r: buf.at[slot, r // SUBLANES, :, r % SUBLANES, :], sem.at[slot])

    def wait_gather(slot):
        pltpu.make_async_copy(buf.at[1 - slot], buf.at[slot], sem.at[slot]).wait()

    @pl.when(jnp.logical_and(b == 0, n_used > 0))
    def _():
        for cp in weight_copies(be_ref[0], 0):
            cp.start()

    @pl.when(b == 0)
    def _():
        start_group = gather_into(tokc_ref, 0)
        for _ in range(ISSUE_GROUPS):
            start_group()

    @pl.when(first_ref[b] == 1)
    def _():
        ws = wslot_ref[b]
        for cp in weight_copies(be_ref[b], ws):
            cp.wait()
        wg_s[...] = wst_g[ws].astype(BF16)
        wu_s[...] = wst_u[ws].astype(BF16)
        wd_s[...] = wst_d[ws].astype(BF16)
        nxt = nxt_ref[b]

        @pl.when(nxt >= 0)
        def _():
            for cp in weight_copies(nxt, 1 - ws):
                cp.start()

    @pl.when(b < n_used)
    def _():
        slot = b % 2
        wait_gather(slot)
        start_group = gather_into(tokn_ref, 1 - slot)
        for c in range(ROW_TILES):
            xs[:, c * LANES:(c + 1) * LANES] = buf[slot, :, c].reshape(MOE_BLOCK, LANES).astype(BF16)
            if c % 4 == 3:
                start_group()
        for j in range(EXPERT_DIM // MXU_N):
            cols = slice(j * MXU_N, (j + 1) * MXU_N)
            g = jnp.dot(xs[...], wg_s[:, cols], preferred_element_type=F32)
            start_group()
            u = jnp.dot(xs[...], wu_s[:, cols], preferred_element_type=F32)
            start_group()
            hs[:, cols] = (_silu(g) * u).astype(BF16)
        for n in range(D // MXU_N):
            yv = jnp.dot(hs[...], wd_s[:, n * MXU_N:(n + 1) * MXU_N], preferred_element_type=F32)
            for cc in range(MXU_N // LANES):
                c = n * (MXU_N // LANES) + cc
                y_ref[pl.ds(c, MOE_BLOCK, stride=ROW_TILES), :] = yv[:, cc * LANES:(cc + 1) * LANES]
            start_group()

    @pl.when(b == n_used)
    def _():
        wait_gather(b % 2)

    @pl.when(b >= n_used)
    def _():
        y_ref[...] = jnp.zeros(y_ref.shape, F32)


def _moe(block_expert, n_used, first, wslot, nxt, slot_tok, h2t, wg, wu, wd):
    tok3 = slot_tok.reshape(N_BLOCKS, 1, MOE_BLOCK)
    smem = pltpu.SMEM
    cur = lambda b, *_: (b, 0, 0)
    nxt_blk = lambda b, *_: (jnp.minimum(b + 1, N_BLOCKS - 1), 0, 0)
    any_spec = pl.BlockSpec(memory_space=pl.ANY)
    return pl.pallas_call(
        _moe_kernel,
        out_shape=jax.ShapeDtypeStruct((N_SLOTS * ROW_TILES, LANES), F32),
        grid_spec=pltpu.PrefetchScalarGridSpec(
            num_scalar_prefetch=5, grid=(N_BLOCKS,),
            in_specs=[pl.BlockSpec((1, 1, MOE_BLOCK), cur, memory_space=smem),
                      pl.BlockSpec((1, 1, MOE_BLOCK), nxt_blk, memory_space=smem),
                      any_spec, any_spec, any_spec, any_spec],
            out_specs=pl.BlockSpec((MOE_BLOCK * ROW_TILES, LANES), lambda b, *_: (b, 0)),
            scratch_shapes=[pltpu.VMEM((2, MOE_BLOCK // SUBLANES, ROW_TILES, SUBLANES, LANES), F32),
                            pltpu.SemaphoreType.DMA((2,)),
                            pltpu.VMEM((2, D, EXPERT_DIM), F32),
                            pltpu.VMEM((2, D, EXPERT_DIM), F32),
                            pltpu.VMEM((2, EXPERT_DIM, D), F32),
                            pltpu.SemaphoreType.DMA((2, 3)),
                            pltpu.VMEM((D, EXPERT_DIM), BF16),
                            pltpu.VMEM((D, EXPERT_DIM), BF16),
                            pltpu.VMEM((EXPERT_DIM, D), BF16),
                            pltpu.VMEM((MOE_BLOCK, D), BF16),
                            pltpu.VMEM((MOE_BLOCK, EXPERT_DIM), BF16)]),
        compiler_params=_cparams(("arbitrary",), 58),
        name="moe",
    )(block_expert, n_used, first, wslot, nxt, tok3, tok3, h2t, wg, wu, wd)


FIN_T = 128


def _final_kernel(slc_ref, sln_ref, y_hbm, gt_ref, x1_ref, h2_ref, mod_ref, wsg_ref, wsu_ref, wsd_ref,
                  wf_ref, o_ref, buf, sem, hs, x2_s):
    i = pl.program_id(0)
    n_steps = pl.num_programs(0)

    def gather_into(sl_ref, slot):
        def dst(q):
            t, k = q // TOP_K, q % TOP_K
            return buf.at[slot, k, t // SUBLANES, :, t % SUBLANES, :]
        return _start_row_gather(y_hbm, sl_ref, FIN_T * TOP_K, dst, sem.at[slot])

    def wait_gather(slot):
        pltpu.make_async_copy(buf.at[1 - slot], buf.at[slot], sem.at[slot]).wait()

    @pl.when(i == 0)
    def _():
        start_group = gather_into(slc_ref, 0)
        for _ in range(ISSUE_GROUPS):
            start_group()

    slot = i % 2
    start_group = gather_into(sln_ref, 1 - slot)

    h = _load_row_tiles(h2_ref, FIN_T).astype(BF16)
    for j in range(SHARED_DIM // MXU_N):
        cols = slice(j * MXU_N, (j + 1) * MXU_N)
        sg = jnp.dot(h, wsg_ref[:, cols], preferred_element_type=F32)
        start_group()
        su = jnp.dot(h, wsu_ref[:, cols], preferred_element_type=F32)
        start_group()
        hs[:, cols] = (_silu(sg) * su).astype(BF16)

    wait_gather(slot)
    gt = gt_ref[...]
    m = mod_ref[0]
    ssq = jnp.zeros((FIN_T, 1), F32)
    for n in range(D // MXU_N):
        cols = slice(n * MXU_N, (n + 1) * MXU_N)
        acc = jnp.dot(hs[...], wsd_ref[:, cols], preferred_element_type=F32)
        for k in range(TOP_K):
            yk = jnp.concatenate(
                [buf[slot, k, :, n * (MXU_N // LANES) + cc].reshape(FIN_T, LANES) for cc in range(MXU_N // LANES)],
                axis=1)
            acc = acc + gt[:, k:k + 1] * yk
        x2 = x1_ref[:, cols] + m[5:6, cols] * acc
        x2_s[:, cols] = x2
        ssq = ssq + jnp.sum(x2 * x2, axis=-1, keepdims=True)
        start_group()
        if n % 2 == 1:
            start_group()
    o_ref[...] = x2_s[...] * lax.rsqrt(ssq * (1.0 / D) + NORM_EPS) * wf_ref[...]

    @pl.when(i == n_steps - 1)
    def _():
        wait_gather(1 - slot)


def _final(slot_tk, y_t, gates_t, x1, h2t, mod, wsg, wsu, wsd, wf):
    n_steps = N // FIN_T
    per_b = S // FIN_T
    sl3 = slot_tk.reshape(n_steps, 1, FIN_T * TOP_K)
    smem = pltpu.SMEM
    row = lambda i: (i, 0)
    const = lambda i: (0, 0)
    return pl.pallas_call(
        _final_kernel,
        out_shape=jax.ShapeDtypeStruct((N, D), F32),
        grid=(n_steps,),
        in_specs=[pl.BlockSpec((1, 1, FIN_T * TOP_K), lambda i: (i, 0, 0), memory_space=smem),
                  pl.BlockSpec((1, 1, FIN_T * TOP_K),
                               lambda i: (jnp.minimum(i + 1, n_steps - 1), 0, 0), memory_space=smem),
                  pl.BlockSpec(memory_space=pl.ANY),
                  pl.BlockSpec((FIN_T, TOP_K), row),
                  pl.BlockSpec((FIN_T, D), row),
                  pl.BlockSpec((FIN_T * ROW_TILES, LANES), row),
                  pl.BlockSpec((1, N_MOD, D), lambda i: (i // per_b, 0, 0)),
                  pl.BlockSpec((D, SHARED_DIM), const),
                  pl.BlockSpec((D, SHARED_DIM), const),
                  pl.BlockSpec((SHARED_DIM, D), const),
                  pl.BlockSpec((1, D), const)],
        out_specs=pl.BlockSpec((FIN_T, D), row),
        scratch_shapes=[pltpu.VMEM((2, TOP_K, FIN_T // SUBLANES, ROW_TILES, SUBLANES, LANES), F32),
                        pltpu.SemaphoreType.DMA((2,)),
                        pltpu.VMEM((FIN_T, SHARED_DIM), BF16),
                        pltpu.VMEM((FIN_T, D), F32)],
        compiler_params=_cparams(("arbitrary",), 48),
        name="final",
    )(sl3, sl3, y_t, gates_t, x1, h2t, mod, wsg, wsu, wsd, wf.reshape(1, D))


def kernel(x, c, w_ada, b_ada, norm1_w, w_in, pool_w, pool_scale, w_branch_pool, lambda_q1, lambda_k1,
           lambda_q2, lambda_k2, subln_w, w_branch_attn, w_out, norm2_w, w_router, router_bias,
           w_gate_e, w_up_e, w_down_e, w_sh_gate, w_sh_up, w_sh_down, final_norm_w):
    assert x.shape == (B, S, D) and w_ada.shape[0] == 1
    x2 = x.reshape(N, D)
    mod = _ada(c, w_ada[0], b_ada[0]).reshape(B, N_MOD, D)

    h1 = _norm1(x2, norm1_w[0], mod)
    proj = _inproj(h1, w_in[0].astype(BF16), _rope_tables())
    y_pool = _pool(proj, pool_w[0], pool_scale[0])
    lam_params = jnp.stack([lambda_q1[0], lambda_k1[0], lambda_q2[0], lambda_k2[0]])
    o_attn = _attn(proj, lam_params, subln_w[0])
    x1, h2t = _merge(y_pool, o_attn, proj, x2, mod, norm2_w[0],
                     w_branch_pool[0].astype(BF16), w_branch_attn[0].astype(BF16),
                     w_out[0].astype(BF16))

    idx, gates, pos, cnt = _router(h2t, w_router[0].T, router_bias[0])
    counts = cnt[:, 0].astype(I32)
    padded = (counts + MOE_BLOCK - 1) // MOE_BLOCK * MOE_BLOCK
    pad_end = jnp.cumsum(padded)
    pad_start = pad_end - padded
    n_used = (pad_end[-1:] // MOE_BLOCK).astype(I32)
    block_row0 = jnp.arange(N_BLOCKS, dtype=I32) * MOE_BLOCK
    block_expert = jnp.minimum(jnp.sum((pad_end[None, :] <= block_row0[:, None]).astype(I32), axis=1),
                               N_EXPERTS - 1).astype(I32)
    slot = _slots(pad_start.astype(I32), idx, pos)
    tok_rows = jnp.broadcast_to(jnp.arange(N, dtype=I32)[None, :] * ROW_TILES, (TOP_K, N))
    slot_tok = jnp.zeros((N_SLOTS,), I32).at[slot.reshape(-1)].set(
        tok_rows.reshape(-1), unique_indices=True, mode="promise_in_bounds")

    bidx = jnp.arange(N_BLOCKS, dtype=I32)
    prev_expert = jnp.concatenate([jnp.full((1,), -1, I32), block_expert[:-1]])
    first = jnp.logical_and(bidx < n_used[0], block_expert != prev_expert)
    wslot = ((jnp.cumsum(first.astype(I32)) - 1) % 2).astype(I32)
    later_first = lax.cummin(jnp.where(first, bidx, N_BLOCKS)[::-1])[::-1]
    next_first = jnp.concatenate([later_first[1:], jnp.full((1,), N_BLOCKS, I32)])
    nxt = jnp.where(next_first < N_BLOCKS, block_expert[jnp.minimum(next_first, N_BLOCKS - 1)], -1).astype(I32)

    y_t = _moe(block_expert, n_used, first.astype(I32), wslot, nxt, slot_tok, h2t,
               w_gate_e[0], w_up_e[0], w_down_e[0])
    out = _final((slot.T * ROW_TILES).reshape(-1), y_t, gates.T, x1, h2t, mod,
                 w_sh_gate[0].astype(BF16), w_sh_up[0].astype(BF16), w_sh_down[0].astype(BF16),
                 final_norm_w)
    return out.reshape(B, S, D)
```

```python
import numpy as np
import jax
import jax.numpy as jnp
from jax import lax
from jax.experimental import pallas as pl
from jax.experimental.pallas import tpu as pltpu

D = 2048
B = 8
S = 2048
N = B * S
POOL_GROUPS = 4
POOL_DIM = 256
POOL_WIDTH = POOL_GROUPS * POOL_DIM
N_HEADS = 8
HEAD_DIM = 128
V_HEAD = 2 * HEAD_DIM
QK_WIDTH = N_HEADS * 2 * HEAD_DIM
ATTN_WIDTH = N_HEADS * V_HEAD
ROT_HALF = HEAD_DIM // 8
ROPE_THETA = 500000.0
ATTN_SCALE = HEAD_DIM ** -0.5
SUBLN_EPS = 1e-5
NORM_EPS = 1e-6
LAMBDA_INIT = 0.8 - 0.6 * float(np.exp(-0.3 * 0))
IN_WIDTH = POOL_WIDTH + 2 * QK_WIDTH + ATTN_WIDTH + 2 * D
Q_OFF = POOL_WIDTH
K_OFF = Q_OFF + QK_WIDTH
V_OFF = K_OFF + QK_WIDTH
GP_OFF = V_OFF + ATTN_WIDTH
GA_OFF = GP_OFF + D
N_EXPERTS = 64
EXPERT_DIM = 512
SHARED_DIM = 512
TOP_K = 8
N_GROUPS = 8
TOPK_GROUPS = 4
GROUP_SIZE = N_EXPERTS // N_GROUPS
ROUTED_SCALE = 2.5
MOE_BLOCK = 512
N_MOD = 6
N_SLOTS = (N * TOP_K + N_EXPERTS * (MOE_BLOCK - 1) + MOE_BLOCK - 1) // MOE_BLOCK * MOE_BLOCK
N_BLOCKS = N_SLOTS // MOE_BLOCK
HALF = D // 2

LANES = 128
SUBLANES = 8
MXU_N = 256
V7X_VMEM_LIMIT = 60000 * 1024

BF16 = jnp.bfloat16
F32 = jnp.float32
I32 = jnp.int32


def _cparams(sem, vmem_mb):
    return pltpu.CompilerParams(dimension_semantics=sem,
                                vmem_limit_bytes=min(vmem_mb * 1024 * 1024, V7X_VMEM_LIMIT))


ROW_TILES = D // LANES


def _store_row_tiles(ref, vals):
    rows = vals.shape[0]
    for c in range(ROW_TILES):
        ref[pl.ds(c, rows, stride=ROW_TILES), :] = vals[:, c * LANES:(c + 1) * LANES]


def _load_row_tiles(ref, rows):
    return jnp.concatenate([ref[pl.ds(c, rows, stride=ROW_TILES), :] for c in range(ROW_TILES)], axis=1)


def _silu(x):
    return x * jax.nn.sigmoid(x)


ADA_TN = 512


def _ada_kernel(c_ref, w_ref, b_ref, o_ref):
    c = c_ref[...]
    ca = _silu(c).astype(BF16)
    o_ref[...] = jnp.dot(ca, w_ref[...].astype(BF16), preferred_element_type=F32) + b_ref[...]


def _ada(c, w_ada, b_ada):
    width = N_MOD * D
    return pl.pallas_call(
        _ada_kernel,
        out_shape=jax.ShapeDtypeStruct((B, width), F32),
        grid=(width // ADA_TN,),
        in_specs=[pl.BlockSpec((B, D), lambda j: (0, 0)),
                  pl.BlockSpec((D, ADA_TN), lambda j: (0, j)),
                  pl.BlockSpec((1, ADA_TN), lambda j: (0, j))],
        out_specs=pl.BlockSpec((B, ADA_TN), lambda j: (0, j)),
        compiler_params=_cparams(("arbitrary",), 24),
        name="ada",
    )(c, w_ada, b_ada.reshape(1, width))


NORM_TR = 512


def _norm1_kernel(x_ref, w_ref, mod_ref, h_ref):
    x = x_ref[...]
    ms = jnp.mean(x * x, axis=-1, keepdims=True)
    y = x * lax.rsqrt(ms + NORM_EPS) * w_ref[...]
    m = mod_ref[0]
    h_ref[...] = (y * (1.0 + m[1:2]) + m[0:1]).astype(BF16)


def _norm1(x2, norm_w, mod):
    per_b = S // NORM_TR
    return pl.pallas_call(
        _norm1_kernel,
        out_shape=jax.ShapeDtypeStruct((N, D), BF16),
        grid=(N // NORM_TR,),
        in_specs=[pl.BlockSpec((NORM_TR, D), lambda i: (i, 0)),
                  pl.BlockSpec((1, D), lambda i: (0, 0)),
                  pl.BlockSpec((1, N_MOD, D), lambda i: (i // per_b, 0, 0))],
        out_specs=pl.BlockSpec((NORM_TR, D), lambda i: (i, 0)),
        compiler_params=_cparams(("arbitrary",), 32),
        name="norm1",
    )(x2, norm_w.reshape(1, D), mod)


INP_TM = 2048
INP_TN = 1024
ROPE_J0 = Q_OFF // INP_TN
ROPE_JK = K_OFF // INP_TN
ROPE_J1 = V_OFF // INP_TN


def _rope_tables():
    pos = np.arange(S, dtype=np.float64)
    inv_freq = 1.0 / (ROPE_THETA ** (np.arange(0, 2 * ROT_HALF, 2, dtype=np.float64) / (2 * ROT_HALF)))
    ang = pos[:, None] * inv_freq[None, :]
    cos, sin = np.cos(ang), np.sin(ang)
    t0 = np.ones((S, HEAD_DIM)); t1 = np.zeros((S, HEAD_DIM)); t2 = np.zeros((S, HEAD_DIM))
    t0[:, :ROT_HALF] = cos; t0[:, ROT_HALF:2 * ROT_HALF] = cos
    t1[:, ROT_HALF:2 * ROT_HALF] = sin
    t2[:, :ROT_HALF] = -sin
    k_tab = np.stack([t0, t1, t2])
    return jnp.asarray(np.stack([k_tab * (ATTN_SCALE * np.log2(np.e)), k_tab]), dtype=F32)


def _inproj_kernel(h_ref, w_ref, tab_ref, o_ref):
    j = pl.program_id(1)
    acc = jnp.dot(h_ref[...], w_ref[...], preferred_element_type=F32)
    is_rope = jnp.logical_and(j >= ROPE_J0, j < ROPE_J1)

    @pl.when(is_rope)
    def _():
        t0 = tab_ref[0, 0]
        t1 = tab_ref[0, 1]
        t2 = tab_ref[0, 2]
        for s in range(INP_TN // HEAD_DIM):
            a = acc[:, s * HEAD_DIM:(s + 1) * HEAD_DIM]
            r = a * t0 + pltpu.roll(a, ROT_HALF, 1) * t1 + pltpu.roll(a, HEAD_DIM - ROT_HALF, 1) * t2
            o_ref[:, s * HEAD_DIM:(s + 1) * HEAD_DIM] = r.astype(BF16)

    @pl.when(jnp.logical_not(is_rope))
    def _():
        o_ref[...] = acc.astype(BF16)


def _inproj(h1, w_in_bf, tabs):
    per_b = S // INP_TM
    return pl.pallas_call(
        _inproj_kernel,
        out_shape=jax.ShapeDtypeStruct((N, IN_WIDTH), BF16),
        grid=(N // INP_TM, IN_WIDTH // INP_TN),
        in_specs=[pl.BlockSpec((INP_TM, D), lambda i, j: (i, 0)),
                  pl.BlockSpec((D, INP_TN), lambda i, j: (0, j)),
                  pl.BlockSpec((1, 3, INP_TM, HEAD_DIM),
                               lambda i, j: (jnp.where(j >= ROPE_JK, 1, 0), 0, i % per_b, 0))],
        out_specs=pl.BlockSpec((INP_TM, INP_TN), lambda i, j: (i, j)),
        compiler_params=_cparams(("arbitrary", "arbitrary"), 58),
        name="inproj",
    )(h1, w_in_bf, tabs)


POOL_PAD = 16


def _pool_kernel(u_ref, w_ref, sc_ref, o_ref, pad_ref):
    g = pl.program_id(1)
    u = u_ref[...].astype(F32)
    pad_ref[0:POOL_PAD, :] = jnp.zeros((POOL_PAD, POOL_DIM), F32)
    sums = []
    s = u
    for sh in (1, 2, 4, 8):
        pad_ref[POOL_PAD:POOL_PAD + S, :] = s
        s = s + pad_ref[POOL_PAD - sh:POOL_PAD - sh + S, :]
        sums.append(s)
    win_sum = jnp.where(g == 0, sums[0], jnp.where(g == 1, sums[1], jnp.where(g == 2, sums[2], sums[3])))
    win = jnp.left_shift(2, g)
    row = lax.broadcasted_iota(I32, (S, POOL_DIM), 0)
    cnt = jnp.minimum(row + 1, win).astype(F32)
    mixed = (win_sum / cnt - u).astype(BF16)
    y = jnp.dot(mixed, w_ref[0].astype(BF16), preferred_element_type=F32) * sc_ref[0]
    o_ref[...] = y.astype(BF16)


def _pool(proj, pool_w, pool_scale):
    return pl.pallas_call(
        _pool_kernel,
        out_shape=jax.ShapeDtypeStruct((N, POOL_WIDTH), BF16),
        grid=(B, POOL_GROUPS),
        in_specs=[pl.BlockSpec((S, POOL_DIM), lambda b, g: (b, g)),
                  pl.BlockSpec((1, POOL_DIM, POOL_DIM), lambda b, g: (g, 0, 0)),
                  pl.BlockSpec((1, 1, POOL_DIM), lambda b, g: (g, 0, 0))],
        out_specs=pl.BlockSpec((S, POOL_DIM), lambda b, g: (b, g)),
        scratch_shapes=[pltpu.VMEM((POOL_PAD + S, POOL_DIM), F32)],
        compiler_params=_cparams(("arbitrary", "arbitrary"), 40),
        name="pool",
    )(proj, pool_w, pool_scale.reshape(POOL_GROUPS, 1, POOL_DIM))


ATT_T = 512


def _attn_kernel(q_ref, k_ref, v_ref, lam_ref, sw_ref, o_ref, s_ref, mx_ref, ls_ref, acc_ref):
    qi = pl.program_id(2)
    q = q_ref[...]
    qs = (q[:, :HEAD_DIM], q[:, HEAD_DIM:])
    mx_ref[...] = jnp.full(mx_ref.shape, -jnp.inf, F32)
    ls_ref[...] = jnp.zeros(ls_ref.shape, F32)
    acc_ref[...] = jnp.zeros(acc_ref.shape, F32)

    def scores(j, masked):
        start = pl.multiple_of(j * ATT_T, ATT_T)
        ks = k_ref[pl.ds(start, ATT_T), :]
        for c in range(2):
            s = lax.dot_general(qs[c], ks[:, c * HEAD_DIM:(c + 1) * HEAD_DIM],
                                (((1,), (1,)), ((), ())), preferred_element_type=F32)
            if masked:
                row = lax.broadcasted_iota(I32, (ATT_T, ATT_T), 0)
                col = lax.broadcasted_iota(I32, (ATT_T, ATT_T), 1)
                s = jnp.where(col <= row, s, -jnp.inf)
            s_ref[c, j] = s
            mx = mx_ref[c]
            for t in range(ATT_T // LANES):
                mx = jnp.maximum(mx, s[:, t * LANES:(t + 1) * LANES])
            mx_ref[c] = mx

    def scores_body(j, carry):
        scores(j, False)
        return carry

    lax.fori_loop(0, qi, scores_body, 0)
    scores(qi, True)

    row_max = [jnp.max(mx_ref[c], axis=-1, keepdims=True) for c in range(2)]

    def pv_body(j, carry):
        start = pl.multiple_of(j * ATT_T, ATT_T)
        vs = v_ref[pl.ds(start, ATT_T), :]
        for c in range(2):
            p = jnp.exp2(s_ref[c, j] - row_max[c])
            ls = ls_ref[c]
            for t in range(ATT_T // LANES):
                ls = ls + p[:, t * LANES:(t + 1) * LANES]
            ls_ref[c] = ls
            acc_ref[c] += jnp.dot(p.astype(BF16), vs, preferred_element_type=F32)
        return carry

    lax.fori_loop(0, qi + 1, pv_body, 0)

    lp = lam_ref[...]
    lam = (jnp.exp(jnp.sum(lp[0:1] * lp[1:2], axis=-1, keepdims=True))
           - jnp.exp(jnp.sum(lp[2:3] * lp[3:4], axis=-1, keepdims=True)) + LAMBDA_INIT)
    l1 = jnp.sum(ls_ref[0], axis=-1, keepdims=True)
    l2 = jnp.sum(ls_ref[1], axis=-1, keepdims=True)
    o = acc_ref[0] / l1 - lam * (acc_ref[1] / l2)
    ms = jnp.mean(o * o, axis=-1, keepdims=True)
    o = o * lax.rsqrt(ms + SUBLN_EPS) * sw_ref[...] * (1.0 - LAMBDA_INIT)
    o_ref[...] = o.astype(BF16)


def _attn(proj, lam_params, subln_w):
    nq = S // ATT_T
    qb, kb, vb = Q_OFF // V_HEAD, K_OFF // V_HEAD, V_OFF // V_HEAD
    return pl.pallas_call(
        _attn_kernel,
        out_shape=jax.ShapeDtypeStruct((N, ATTN_WIDTH), BF16),
        grid=(B, N_HEADS, nq),
        in_specs=[pl.BlockSpec((ATT_T, V_HEAD), lambda b, h, i: (b * nq + i, qb + h)),
                  pl.BlockSpec((S, V_HEAD), lambda b, h, i: (b, kb + h)),
                  pl.BlockSpec((S, V_HEAD), lambda b, h, i: (b, vb + h)),
                  pl.BlockSpec((4, HEAD_DIM), lambda b, h, i: (0, 0)),
                  pl.BlockSpec((1, V_HEAD), lambda b, h, i: (0, 0))],
        out_specs=pl.BlockSpec((ATT_T, V_HEAD), lambda b, h, i: (b * nq + i, h)),
        scratch_shapes=[pltpu.VMEM((2, S // ATT_T, ATT_T, ATT_T), F32),
                        pltpu.VMEM((2, ATT_T, LANES), F32),
                        pltpu.VMEM((2, ATT_T, LANES), F32),
                        pltpu.VMEM((2, ATT_T, V_HEAD), F32)],
        compiler_params=_cparams(("arbitrary", "arbitrary", "arbitrary"), 40),
        name="attn",
    )(proj, proj, proj, lam_params, subln_w.reshape(1, V_HEAD))


MRG_TM = 256


def _merge_kernel(yp_ref, o_ref, gp0_ref, gp1_ref, ga0_ref, ga1_ref, x_ref, mod_ref, w2_ref,
                  wbp_ref, wba_ref, wout_ref, x1_ref, h2t_ref):
    a1 = jnp.dot(yp_ref[...], wbp_ref[...], preferred_element_type=F32)
    a2 = jnp.dot(o_ref[...], wba_ref[...], preferred_element_type=F32)
    m_lo = (jax.nn.sigmoid(gp0_ref[...].astype(F32)) * a1[:, :HALF]
            + jax.nn.sigmoid(ga0_ref[...].astype(F32)) * a2[:, :HALF]).astype(BF16)
    m_hi = (jax.nn.sigmoid(gp1_ref[...].astype(F32)) * a1[:, HALF:]
            + jax.nn.sigmoid(ga1_ref[...].astype(F32)) * a2[:, HALF:]).astype(BF16)
    out = (jnp.dot(m_lo, wout_ref[0:HALF, :], preferred_element_type=F32)
           + jnp.dot(m_hi, wout_ref[HALF:D, :], preferred_element_type=F32))
    m = mod_ref[0]
    x1 = x_ref[...] + m[2:3] * out
    x1_ref[...] = x1
    ms = jnp.mean(x1 * x1, axis=-1, keepdims=True)
    h2 = x1 * lax.rsqrt(ms + NORM_EPS) * w2_ref[...] * (1.0 + m[4:5]) + m[3:4]
    _store_row_tiles(h2t_ref, h2)


def _merge(y_pool, o_attn, proj, x2, mod, norm2_w, wbp, wba, wout):
    per_b = S // MRG_TM
    gpb, gab = GP_OFF // HALF, GA_OFF // HALF
    row = lambda i: (i, 0)
    const = lambda i: (0, 0)
    wspec = lambda shape: pl.BlockSpec(shape, const, pipeline_mode=pl.Buffered(1))
    return pl.pallas_call(
        _merge_kernel,
        out_shape=(jax.ShapeDtypeStruct((N, D), F32),
                   jax.ShapeDtypeStruct((N * ROW_TILES, LANES), F32)),
        grid=(N // MRG_TM,),
        in_specs=[pl.BlockSpec((MRG_TM, POOL_WIDTH), row),
                  pl.BlockSpec((MRG_TM, ATTN_WIDTH), row),
                  pl.BlockSpec((MRG_TM, HALF), lambda i: (i, gpb)),
                  pl.BlockSpec((MRG_TM, HALF), lambda i: (i, gpb + 1)),
                  pl.BlockSpec((MRG_TM, HALF), lambda i: (i, gab)),
                  pl.BlockSpec((MRG_TM, HALF), lambda i: (i, gab + 1)),
                  pl.BlockSpec((MRG_TM, D), row),
                  pl.BlockSpec((1, N_MOD, D), lambda i: (i // per_b, 0, 0)),
                  pl.BlockSpec((1, D), const),
                  wspec((POOL_WIDTH, D)),
                  wspec((ATTN_WIDTH, D)),
                  wspec((D, D))],
        out_specs=(pl.BlockSpec((MRG_TM, D), row),
                   pl.BlockSpec((MRG_TM * ROW_TILES, LANES), row)),
        compiler_params=_cparams(("arbitrary",), 58),
        name="merge",
    )(y_pool, o_attn, proj, proj, proj, proj, x2, mod, norm2_w.reshape(1, D), wbp, wba, wout)


RT_T = 512


def _router_kernel(h_ref, wr_ref, bias_ref, idx_ref, gate_ref, pos_ref, cnt_ref, carry_ref):
    i = pl.program_id(0)

    @pl.when(i == 0)
    def _():
        carry_ref[...] = jnp.zeros(carry_ref.shape, F32)

    logits = lax.dot_general(wr_ref[...], _load_row_tiles(h_ref, RT_T), (((1,), (1,)), ((), ())),
                             precision=lax.Precision.HIGHEST, preferred_element_type=F32)
    scores = jax.nn.sigmoid(logits)
    biased = scores + bias_ref[...]
    neg = -jnp.inf

    b3 = biased.reshape(N_GROUPS, GROUP_SIZE, RT_T)
    io3 = lax.broadcasted_iota(I32, b3.shape, 1)
    m1 = jnp.max(b3, axis=1, keepdims=True)
    i1 = jnp.min(jnp.where(b3 == m1, io3, GROUP_SIZE), axis=1, keepdims=True)
    m2 = jnp.max(jnp.where(io3 == i1, neg, b3), axis=1, keepdims=True)
    gs = m1 + m2

    gio = lax.broadcasted_iota(I32, gs.shape, 0)
    gsel = jnp.zeros(gs.shape, jnp.bool_)
    cur = gs
    for _ in range(TOPK_GROUPS):
        mx = jnp.max(cur, axis=0, keepdims=True)
        ix = jnp.min(jnp.where(cur == mx, gio, N_GROUPS), axis=0, keepdims=True)
        pick = gio == ix
        gsel = jnp.logical_or(gsel, pick)
        cur = jnp.where(pick, neg, cur)

    masked = jnp.where(gsel, b3, neg).reshape(N_EXPERTS, RT_T)
    eio = lax.broadcasted_iota(I32, (N_EXPERTS, RT_T), 0)
    assign = jnp.zeros((N_EXPERTS, RT_T), F32)
    idxs, sels = [], []
    for _ in range(TOP_K):
        mx = jnp.max(masked, axis=0, keepdims=True)
        ix = jnp.min(jnp.where(masked == mx, eio, N_EXPERTS), axis=0, keepdims=True)
        hot = eio == ix
        sels.append(jnp.sum(jnp.where(hot, scores, 0.0), axis=0, keepdims=True))
        idxs.append(ix)
        masked = jnp.where(hot, neg, masked)
        assign = jnp.where(hot, 1.0, assign)

    denom = sels[0]
    for k in range(1, TOP_K):
        denom = denom + sels[k]

    tp = lax.broadcasted_iota(I32, (RT_T, RT_T), 0)
    tc = lax.broadcasted_iota(I32, (RT_T, RT_T), 1)
    before = jnp.where(tp < tc, 1.0, 0.0).astype(BF16)
    rank = jnp.dot(assign.astype(BF16), before, preferred_element_type=F32) + carry_ref[...]
    for k in range(TOP_K):
        idx_ref[k:k + 1, :] = idxs[k]
        gate_ref[k:k + 1, :] = sels[k] / denom * ROUTED_SCALE
        pos_ref[k:k + 1, :] = jnp.sum(jnp.where(eio == idxs[k], rank, 0.0), axis=0,
                                      keepdims=True).astype(I32)
    carry = carry_ref[...] + jnp.sum(assign, axis=1, keepdims=True)
    carry_ref[...] = carry
    cnt_ref[...] = jnp.broadcast_to(carry, cnt_ref.shape)


def _router(h2, w_router_t, bias):
    tok = lambda i: (0, i)
    return pl.pallas_call(
        _router_kernel,
        out_shape=(jax.ShapeDtypeStruct((TOP_K, N), I32),
                   jax.ShapeDtypeStruct((TOP_K, N), F32),
                   jax.ShapeDtypeStruct((TOP_K, N), I32),
                   jax.ShapeDtypeStruct((N_EXPERTS, LANES), F32)),
        grid=(N // RT_T,),
        in_specs=[pl.BlockSpec((RT_T * ROW_TILES, LANES), lambda i: (i, 0)),
                  pl.BlockSpec((N_EXPERTS, D), lambda i: (0, 0)),
                  pl.BlockSpec((N_EXPERTS, 1), lambda i: (0, 0))],
        out_specs=(pl.BlockSpec((TOP_K, RT_T), tok),
                   pl.BlockSpec((TOP_K, RT_T), tok),
                   pl.BlockSpec((TOP_K, RT_T), tok),
                   pl.BlockSpec((N_EXPERTS, LANES), lambda i: (0, 0))),
        scratch_shapes=[pltpu.VMEM((N_EXPERTS, 1), F32)],
        compiler_params=_cparams(("arbitrary",), 32),
        name="router",
    )(h2, w_router_t, bias.reshape(N_EXPERTS, 1))


SLOT_T = 2048


def _slot_kernel(ps_ref, idx_ref, pos_ref, slot_ref):
    idx = idx_ref[...]
    base = jnp.zeros(idx.shape, I32)
    for e in range(N_EXPERTS):
        base = jnp.where(idx == e, ps_ref[e], base)
    slot_ref[...] = base + pos_ref[...]


def _slots(pad_start, idx, pos):
    tok = lambda i, ps: (0, i)
    return pl.pallas_call(
        _slot_kernel,
        out_shape=jax.ShapeDtypeStruct((TOP_K, N), I32),
        grid_spec=pltpu.PrefetchScalarGridSpec(
            num_scalar_prefetch=1, grid=(N // SLOT_T,),
            in_specs=[pl.BlockSpec((TOP_K, SLOT_T), tok), pl.BlockSpec((TOP_K, SLOT_T), tok)],
            out_specs=pl.BlockSpec((TOP_K, SLOT_T), tok)),
        compiler_params=_cparams(("arbitrary",), 16),
        name="slots",
    )(pad_start, idx, pos)


ISSUE_GROUPS = 16


def _start_row_gather(src_hbm, off_ref, n_rows, dst_of_row, sem):
    per_group = n_rows // ISSUE_GROUPS
    state = {"next": 0}

    def start_group():
        g = state["next"]
        state["next"] = g + 1
        for r in range(g * per_group, (g + 1) * per_group):
            src = pl.multiple_of(off_ref[0, 0, r], ROW_TILES)
            pltpu.make_async_copy(src_hbm.at[pl.ds(src, ROW_TILES), :], dst_of_row(r), sem).start(
                priority=r % 2)
    return start_group


def _moe_kernel(be_ref, nu_ref, first_ref, wslot_ref, nxt_ref, tokc_ref, tokn_ref, h_hbm,
                wg_hbm, wu_hbm, wd_hbm, y_ref, buf, sem, wst_g, wst_u, wst_d, wsem, wg_s, wu_s, wd_s, xs, hs):
    b = pl.program_id(0)
    n_used = nu_ref[0]

    def weight_copies(e, ws):
        return (pltpu.make_async_copy(wg_hbm.at[e], wst_g.at[ws], wsem.at[ws, 0]),
                pltpu.make_async_copy(wu_hbm.at[e], wst_u.at[ws], wsem.at[ws, 1]),
                pltpu.make_async_copy(wd_hbm.at[e], wst_d.at[ws], wsem.at[ws, 2]))

    def gather_into(tok_ref, slot):
        return _start_row_gather(h_hbm, tok_ref, MOE_BLOCK,
                                 lambda r: buf.at[slot, r // SUBLANES, :, r % SUBLANES, :], sem.at[slot])

    def wait_gather(slot):
        pltpu.make_async_copy(buf.at[1 - slot], buf.at[slot], sem.at[slot]).wait()

    @pl.when(jnp.logical_and(b == 0, n_used > 0))
    def _():
        for cp in weight_copies(be_ref[0], 0):
            cp.start()

    @pl.when(b == 0)
    def _():
        start_group = gather_into(tokc_ref, 0)
        for _ in range(ISSUE_GROUPS):
            start_group()

    @pl.when(first_ref[b] == 1)
    def _():
        ws = wslot_ref[b]
        for cp in weight_copies(be_ref[b], ws):
            cp.wait()
        wg_s[...] = wst_g[ws].astype(BF16)
        wu_s[...] = wst_u[ws].astype(BF16)
        wd_s[...] = wst_d[ws].astype(BF16)
        nxt = nxt_ref[b]

        @pl.when(nxt >= 0)
        def _():
            for cp in weight_copies(nxt, 1 - ws):
                cp.start()

    @pl.when(b < n_used)
    def _():
        slot = b % 2
        wait_gather(slot)
        start_group = gather_into(tokn_ref, 1 - slot)
        for c in range(ROW_TILES):
            xs[:, c * LANES:(c + 1) * LANES] = buf[slot, :, c].reshape(MOE_BLOCK, LANES).astype(BF16)
            if c % 4 == 3:
                start_group()
        for j in range(EXPERT_DIM // MXU_N):
            cols = slice(j * MXU_N, (j + 1) * MXU_N)
            g = jnp.dot(xs[...], wg_s[:, cols], preferred_element_type=F32)
            start_group()
            u = jnp.dot(xs[...], wu_s[:, cols], preferred_element_type=F32)
            start_group()
            hs[:, cols] = (_silu(g) * u).astype(BF16)
        for n in range(D // MXU_N):
            yv = jnp.dot(hs[...], wd_s[:, n * MXU_N:(n + 1) * MXU_N], preferred_element_type=F32)
            for cc in range(MXU_N // LANES):
                c = n * (MXU_N // LANES) + cc
                y_ref[pl.ds(c, MOE_BLOCK, stride=ROW_TILES), :] = yv[:, cc * LANES:(cc + 1) * LANES]
            start_group()

    @pl.when(b == n_used)
    def _():
        wait_gather(b % 2)

    @pl.when(b >= n_used)
    def _():
        y_ref[...] = jnp.zeros(y_ref.shape, F32)


def _moe(block_expert, n_used, first, wslot, nxt, slot_tok, h2t, wg, wu, wd):
    tok3 = slot_tok.reshape(N_BLOCKS, 1, MOE_BLOCK)
    smem = pltpu.SMEM
    cur = lambda b, *_: (b, 0, 0)
    nxt_blk = lambda b, *_: (jnp.minimum(b + 1, N_BLOCKS - 1), 0, 0)
    any_spec = pl.BlockSpec(memory_space=pl.ANY)
    return pl.pallas_call(
        _moe_kernel,
        out_shape=jax.ShapeDtypeStruct((N_SLOTS * ROW_TILES, LANES), F32),
        grid_spec=pltpu.PrefetchScalarGridSpec(
            num_scalar_prefetch=5, grid=(N_BLOCKS,),
            in_specs=[pl.BlockSpec((1, 1, MOE_BLOCK), cur, memory_space=smem),
                      pl.BlockSpec((1, 1, MOE_BLOCK), nxt_blk, memory_space=smem),
                      any_spec, any_spec, any_spec, any_spec],
            out_specs=pl.BlockSpec((MOE_BLOCK * ROW_TILES, LANES), lambda b, *_: (b, 0)),
            scratch_shapes=[pltpu.VMEM((2, MOE_BLOCK // SUBLANES, ROW_TILES, SUBLANES, LANES), F32),
                            pltpu.SemaphoreType.DMA((2,)),
                            pltpu.VMEM((2, D, EXPERT_DIM), F32),
                            pltpu.VMEM((2, D, EXPERT_DIM), F32),
                            pltpu.VMEM((2, EXPERT_DIM, D), F32),
                            pltpu.SemaphoreType.DMA((2, 3)),
                            pltpu.VMEM((D, EXPERT_DIM), BF16),
                            pltpu.VMEM((D, EXPERT_DIM), BF16),
                            pltpu.VMEM((EXPERT_DIM, D), BF16),
                            pltpu.VMEM((MOE_BLOCK, D), BF16),
                            pltpu.VMEM((MOE_BLOCK, EXPERT_DIM), BF16)]),
        compiler_params=_cparams(("arbitrary",), 58),
        name="moe",
    )(block_expert, n_used, first, wslot, nxt, tok3, tok3, h2t, wg, wu, wd)


FIN_T = 128


def _final_kernel(slc_ref, sln_ref, y_hbm, gt_ref, x1_ref, h2_ref, mod_ref, wsg_ref, wsu_ref, wsd_ref,
                  wf_ref, o_ref, buf, sem, hs, x2_s):
    i = pl.program_id(0)
    n_steps = pl.num_programs(0)

    def gather_into(sl_ref, slot):
        def dst(q):
            t, k = q // TOP_K, q % TOP_K
            return buf.at[slot, k, t // SUBLANES, :, t % SUBLANES, :]
        return _start_row_gather(y_hbm, sl_ref, FIN_T * TOP_K, dst, sem.at[slot])

    def wait_gather(slot):
        pltpu.make_async_copy(buf.at[1 - slot], buf.at[slot], sem.at[slot]).wait()

    @pl.when(i == 0)
    def _():
        start_group = gather_into(slc_ref, 0)
        for _ in range(ISSUE_GROUPS):
            start_group()

    slot = i % 2
    start_group = gather_into(sln_ref, 1 - slot)

    h = _load_row_tiles(h2_ref, FIN_T).astype(BF16)
    for j in range(SHARED_DIM // MXU_N):
        cols = slice(j * MXU_N, (j + 1) * MXU_N)
        sg = jnp.dot(h, wsg_ref[:, cols], preferred_element_type=F32)
        start_group()
        su = jnp.dot(h, wsu_ref[:, cols], preferred_element_type=F32)
        start_group()
        hs[:, cols] = (_silu(sg) * su).astype(BF16)

    wait_gather(slot)
    gt = gt_ref[...]
    m = mod_ref[0]
    ssq = jnp.zeros((FIN_T, 1), F32)
    for n in range(D // MXU_N):
        cols = slice(n * MXU_N, (n + 1) * MXU_N)
        acc = jnp.dot(hs[...], wsd_ref[:, cols], preferred_element_type=F32)
        for k in range(TOP_K):
            yk = jnp.concatenate(
                [buf[slot, k, :, n * (MXU_N // LANES) + cc].reshape(FIN_T, LANES) for cc in range(MXU_N // LANES)],
                axis=1)
            acc = acc + gt[:, k:k + 1] * yk
        x2 = x1_ref[:, cols] + m[5:6, cols] * acc
        x2_s[:, cols] = x2
        ssq = ssq + jnp.sum(x2 * x2, axis=-1, keepdims=True)
        start_group()
        if n % 2 == 1:
            start_group()
    o_ref[...] = x2_s[...] * lax.rsqrt(ssq * (1.0 / D) + NORM_EPS) * wf_ref[...]

    @pl.when(i == n_steps - 1)
    def _():
        wait_gather(1 - slot)


def _final(slot_tk, y_t, gates_t, x1, h2t, mod, wsg, wsu, wsd, wf):
    n_steps = N // FIN_T
    per_b = S // FIN_T
    sl3 = slot_tk.reshape(n_steps, 1, FIN_T * TOP_K)
    smem = pltpu.SMEM
    row = lambda i: (i, 0)
    const = lambda i: (0, 0)
    return pl.pallas_call(
        _final_kernel,
        out_shape=jax.ShapeDtypeStruct((N, D), F32),
        grid=(n_steps,),
        in_specs=[pl.BlockSpec((1, 1, FIN_T * TOP_K), lambda i: (i, 0, 0), memory_space=smem),
                  pl.BlockSpec((1, 1, FIN_T * TOP_K),
                               lambda i: (jnp.minimum(i + 1, n_steps - 1), 0, 0), memory_space=smem),
                  pl.BlockSpec(memory_space=pl.ANY),
                  pl.BlockSpec((FIN_T, TOP_K), row),
                  pl.BlockSpec((FIN_T, D), row),
                  pl.BlockSpec((FIN_T * ROW_TILES, LANES), row),
                  pl.BlockSpec((1, N_MOD, D), lambda i: (i // per_b, 0, 0)),
                  pl.BlockSpec((D, SHARED_DIM), const),
                  pl.BlockSpec((D, SHARED_DIM), const),
                  pl.BlockSpec((SHARED_DIM, D), const),
                  pl.BlockSpec((1, D), const)],
        out_specs=pl.BlockSpec((FIN_T, D), row),
        scratch_shapes=[pltpu.VMEM((2, TOP_K, FIN_T // SUBLANES, ROW_TILES, SUBLANES, LANES), F32),
                        pltpu.SemaphoreType.DMA((2,)),
                        pltpu.VMEM((FIN_T, SHARED_DIM), BF16),
                        pltpu.VMEM((FIN_T, D), F32)],
        compiler_params=_cparams(("arbitrary",), 48),
        name="final",
    )(sl3, sl3, y_t, gates_t, x1, h2t, mod, wsg, wsu, wsd, wf.reshape(1, D))


def kernel(x, c, w_ada, b_ada, norm1_w, w_in, pool_w, pool_scale, w_branch_pool, lambda_q1, lambda_k1,
           lambda_q2, lambda_k2, subln_w, w_branch_attn, w_out, norm2_w, w_router, router_bias,
           w_gate_e, w_up_e, w_down_e, w_sh_gate, w_sh_up, w_sh_down, final_norm_w):
    assert x.shape == (B, S, D) and w_ada.shape[0] == 1
    x2 = x.reshape(N, D)
    mod = _ada(c, w_ada[0], b_ada[0]).reshape(B, N_MOD, D)

    h1 = _norm1(x2, norm1_w[0], mod)
    proj = _inproj(h1, w_in[0].astype(BF16), _rope_tables())
    y_pool = _pool(proj, pool_w[0], pool_scale[0])
    lam_params = jnp.stack([lambda_q1[0], lambda_k1[0], lambda_q2[0], lambda_k2[0]])
    o_attn = _attn(proj, lam_params, subln_w[0])
    x1, h2t = _merge(y_pool, o_attn, proj, x2, mod, norm2_w[0],
                     w_branch_pool[0].astype(BF16), w_branch_attn[0].astype(BF16),
                     w_out[0].astype(BF16))

    idx, gates, pos, cnt = _router(h2t, w_router[0].T, router_bias[0])
    counts = cnt[:, 0].astype(I32)
    padded = (counts + MOE_BLOCK - 1) // MOE_BLOCK * MOE_BLOCK
    pad_end = jnp.cumsum(padded)
    pad_start = pad_end - padded
    n_used = (pad_end[-1:] // MOE_BLOCK).astype(I32)
    block_row0 = jnp.arange(N_BLOCKS, dtype=I32) * MOE_BLOCK
    block_expert = jnp.minimum(jnp.sum((pad_end[None, :] <= block_row0[:, None]).astype(I32), axis=1),
                               N_EXPERTS - 1).astype(I32)
    slot = _slots(pad_start.astype(I32), idx, pos)
    tok_rows = jnp.broadcast_to(jnp.arange(N, dtype=I32)[None, :] * ROW_TILES, (TOP_K, N))
    slot_tok = jnp.zeros((N_SLOTS,), I32).at[slot.reshape(-1)].set(
        tok_rows.reshape(-1), unique_indices=True, mode="promise_in_bounds")

    bidx = jnp.arange(N_BLOCKS, dtype=I32)
    prev_expert = jnp.concatenate([jnp.full((1,), -1, I32), block_expert[:-1]])
    first = jnp.logical_and(bidx < n_used[0], block_expert != prev_expert)
    wslot = ((jnp.cumsum(first.astype(I32)) - 1) % 2).astype(I32)
    later_first = lax.cummin(jnp.where(first, bidx, N_BLOCKS)[::-1])[::-1]
    next_first = jnp.concatenate([later_first[1:], jnp.full((1,), N_BLOCKS, I32)])
    nxt = jnp.where(next_first < N_BLOCKS, block_expert[jnp.minimum(next_first, N_BLOCKS - 1)], -1).astype(I32)

    y_t = _moe(block_expert, n_used, first.astype(I32), wslot, nxt, slot_tok, h2t,
               w_gate_e[0], w_up_e[0], w_down_e[0])
    out = _final((slot.T * ROW_TILES).reshape(-1), y_t, gates.T, x1, h2t, mod,
                 w_sh_gate[0].astype(BF16), w_sh_up[0].astype(BF16), w_sh_down[0].astype(BF16),
                 final_norm_w)
    return out.reshape(B, S, D)
```
